```python
import jax, jax.numpy as jnp
from jax import lax
import numpy as np

D_MODEL = 1024
BATCH = 16
SEQ = 2048
DEPTH = 4

CHUNK = 64
N_MIXERS = 2
POOL_WINDOWS = (2, 4, 8, 16)
N_POOL_GROUPS = len(POOL_WINDOWS)
POOL_GROUP = D_MODEL // N_POOL_GROUPS
N_HEADS = 8
HEAD_DIM = D_MODEL // N_HEADS
CONV_WIDTH = 4
D_FF = 4 * D_MODEL
RMS_EPS = 1e-6
PROJ_WIDTH = 4 * D_MODEL + 2 * N_HEADS
N_POOL_LAYERS = (DEPTH + 1) // 2
N_MLSTM_LAYERS = DEPTH // 2

kernel_name = "hybrid_pool_mlstm_sandwich_trunk"


def rmsnorm(x, g):
    xf = x.astype(jnp.float32)
    y = xf * lax.rsqrt(jnp.mean(xf * xf, axis=-1, keepdims=True) + RMS_EPS)
    return (y * g.astype(jnp.float32)).astype(x.dtype)


def pool_mixer(h, w_grp, b, scale):
    B, S, D = h.shape
    hf = h.astype(jnp.float32)
    cs0 = jnp.concatenate([jnp.zeros((B, 1, D), jnp.float32), jnp.cumsum(hf, axis=1)], axis=1)
    t = jnp.arange(S, dtype=jnp.float32)
    parts = []
    for g, w in enumerate(POOL_WINDOWS):
        c = cs0[:, :, g * POOL_GROUP:(g + 1) * POOL_GROUP]
        upper = c[:, 1:]
        lower = jnp.pad(c[:, :S - w + 1], ((0, 0), (w - 1, 0), (0, 0)))
        count = jnp.minimum(t + 1.0, float(w))
        parts.append((upper - lower) / count[None, :, None])
    pooled = jnp.concatenate(parts, axis=-1) - hf
    pooled = pooled.reshape(B, S, N_POOL_GROUPS, POOL_GROUP)
    y = jnp.einsum('bsgc,gcd->bsgd', pooled, w_grp.astype(jnp.float32)).reshape(B, S, D)
    y = (y + b.astype(jnp.float32)) * scale.astype(jnp.float32)
    return y.astype(h.dtype)


def causal_depthwise_conv(u, w, b):
    S = u.shape[1]
    up = jnp.pad(u, ((0, 0), (CONV_WIDTH - 1, 0), (0, 0)))
    out = b.astype(jnp.float32)
    for j in range(CONV_WIDTH):
        out = out + w[j].astype(jnp.float32) * up[:, j:j + S]
    return out


def mlstm_chunk_step(carry, inp):
    C, n, m = carry
    q, k, v, ig, lf = inp
    L = q.shape[2]
    causal = jnp.tril(jnp.ones((L, L), dtype=bool))
    b = jnp.cumsum(lf, axis=-1)
    log_d = b[..., :, None] - b[..., None, :] + ig[..., None, :]
    log_d = jnp.where(causal, log_d, -jnp.inf)
    m_inter = b + m[..., None]
    m_out = jnp.maximum(jnp.max(log_d, axis=-1), m_inter)
    s = jnp.einsum('bhtd,bhsd->bhts', q, k) * jnp.exp(log_d - m_out[..., None])
    inter = jnp.exp(m_inter - m_out)
    num = jnp.einsum('bhts,bhsd->bhtd', s, v) + inter[..., None] * jnp.einsum('bhtk,bhkv->bhtv', q, C)
    den = jnp.sum(s, axis=-1) + inter * jnp.einsum('bhtk,bhk->bht', q, n)
    h = num / jnp.maximum(jnp.abs(den), jnp.exp(-m_out))[..., None]
    g = b[..., -1]
    a = g[..., None] - b + ig
    m_new = jnp.maximum(g + m, jnp.max(a, axis=-1))
    decay = jnp.exp(g + m - m_new)
    wts = jnp.exp(a - m_new[..., None])
    C_new = decay[..., None, None] * C + jnp.einsum('bhs,bhsk,bhsv->bhkv', wts, k, v)
    n_new = decay[..., None] * n + jnp.einsum('bhs,bhsk->bhk', wts, k)
    return (C_new, n_new, m_new), h


def mlstm_mixer(h, w_in, conv_w, conv_b, i_bias, f_bias, head_norm_w, w_out):
    B, S, D = h.shape
    NC = S // CHUNK
    proj = jnp.einsum('bsd,de->bse', h, w_in).astype(jnp.float32)
    qk = proj[..., :2 * D]
    v = proj[..., 2 * D:3 * D]
    o_pre = proj[..., 3 * D:4 * D]
    i_pre = proj[..., 4 * D:4 * D + N_HEADS]
    f_pre = proj[..., 4 * D + N_HEADS:]
    qk = jax.nn.silu(causal_depthwise_conv(qk, conv_w, conv_b))
    q = qk[..., :D]
    k = qk[..., D:] * (HEAD_DIM ** -0.5)
    ig = i_pre + i_bias.astype(jnp.float32)
    lf = jax.nn.log_sigmoid(f_pre + f_bias.astype(jnp.float32))

    def to_chunks(a):
        return a.reshape(B, NC, CHUNK, N_HEADS, HEAD_DIM).transpose(1, 0, 3, 2, 4)

    def gate_chunks(a):
        return a.reshape(B, NC, CHUNK, N_HEADS).transpose(1, 0, 3, 2)

    init = (jnp.zeros((B, N_HEADS, HEAD_DIM, HEAD_DIM), jnp.float32),
            jnp.zeros((B, N_HEADS, HEAD_DIM), jnp.float32),
            jnp.zeros((B, N_HEADS), jnp.float32))
    _, hc = lax.scan(mlstm_chunk_step, init,
                     (to_chunks(q), to_chunks(k), to_chunks(v), gate_chunks(ig), gate_chunks(lf)))
    hs = hc.transpose(1, 0, 3, 2, 4).reshape(B, S, N_HEADS, HEAD_DIM)
    hs = hs * lax.rsqrt(jnp.mean(hs * hs, axis=-1, keepdims=True) + RMS_EPS)
    hs = hs.reshape(B, S, D) * head_norm_w.astype(jnp.float32)
    gated = jax.nn.sigmoid(o_pre) * hs
    return jnp.einsum('bsd,de->bse', gated.astype(h.dtype), w_out)


def sq_relu_mlp(h, w1, w2):
    a = jax.nn.relu(jnp.einsum('bsd,df->bsf', h, w1))
    return jnp.einsum('bsf,fd->bsd', a * a, w2)


def setup_inputs(seed: int = 0) -> dict:
    key = jax.random.key(seed)
    ks = jax.random.split(key, 20)
    nrm = jax.random.normal
    f32 = jnp.float32
    x = nrm(ks[0], (BATCH, SEQ, D_MODEL), f32)
    mix_pre_g = 1.0 + 0.02 * nrm(ks[1], (DEPTH, D_MODEL), f32)
    mix_post_g = 1.0 + 0.02 * nrm(ks[2], (DEPTH, D_MODEL), f32)
    ffn_pre_g = 1.0 + 0.02 * nrm(ks[3], (DEPTH, D_MODEL), f32)
    ffn_post_g = 1.0 + 0.02 * nrm(ks[4], (DEPTH, D_MODEL), f32)
    pool_w = nrm(ks[5], (N_POOL_LAYERS, N_POOL_GROUPS, POOL_GROUP, POOL_GROUP), f32) * POOL_GROUP ** -0.5
    pool_b = 0.02 * nrm(ks[6], (N_POOL_LAYERS, D_MODEL), f32)
    pool_scale = 1.0 + 0.1 * nrm(ks[7], (N_POOL_LAYERS, D_MODEL), f32)
    mlstm_w_in = nrm(ks[8], (N_MLSTM_LAYERS, D_MODEL, PROJ_WIDTH), f32) * D_MODEL ** -0.5
    mlstm_conv_w = nrm(ks[9], (N_MLSTM_LAYERS, CONV_WIDTH, 2 * D_MODEL), f32) * CONV_WIDTH ** -0.5
    mlstm_conv_b = 0.02 * nrm(ks[10], (N_MLSTM_LAYERS, 2 * D_MODEL), f32)
    mlstm_i_bias = 0.1 * nrm(ks[11], (N_MLSTM_LAYERS, N_HEADS), f32)
    mlstm_f_bias = (jnp.linspace(3.0, 6.0, N_HEADS, dtype=f32)[None, :]
                    + 0.1 * nrm(ks[12], (N_MLSTM_LAYERS, N_HEADS), f32))
    mlstm_head_norm_w = 1.0 + 0.02 * nrm(ks[13], (N_MLSTM_LAYERS, D_MODEL), f32)
    mlstm_w_out = nrm(ks[14], (N_MLSTM_LAYERS, D_MODEL, D_MODEL), f32) * D_MODEL ** -0.5
    mlp_w1 = nrm(ks[15], (DEPTH, D_MODEL, D_FF), f32) * D_MODEL ** -0.5
    mlp_w2 = nrm(ks[16], (DEPTH, D_FF, D_MODEL), f32) * D_FF ** -0.5
    return {"x": x, "mix_pre_g": mix_pre_g, "mix_post_g": mix_post_g,
            "ffn_pre_g": ffn_pre_g, "ffn_post_g": ffn_post_g,
            "pool_w": pool_w, "pool_b": pool_b, "pool_scale": pool_scale,
            "mlstm_w_in": mlstm_w_in, "mlstm_conv_w": mlstm_conv_w, "mlstm_conv_b": mlstm_conv_b,
            "mlstm_i_bias": mlstm_i_bias, "mlstm_f_bias": mlstm_f_bias,
            "mlstm_head_norm_w": mlstm_head_norm_w, "mlstm_w_out": mlstm_w_out,
            "mlp_w1": mlp_w1, "mlp_w2": mlp_w2}


def reference(x, mix_pre_g, mix_post_g, ffn_pre_g, ffn_post_g,
              pool_w, pool_b, pool_scale,
              mlstm_w_in, mlstm_conv_w, mlstm_conv_b, mlstm_i_bias, mlstm_f_bias,
              mlstm_head_norm_w, mlstm_w_out, mlp_w1, mlp_w2):
    for i in range(DEPTH):
        j = i // N_MIXERS
        hn = rmsnorm(x, mix_pre_g[i])
        if i % N_MIXERS == 0:
            y = pool_mixer(hn, pool_w[j], pool_b[j], pool_scale[j])
        else:
            y = mlstm_mixer(hn, mlstm_w_in[j], mlstm_conv_w[j], mlstm_conv_b[j],
                            mlstm_i_bias[j], mlstm_f_bias[j], mlstm_head_norm_w[j], mlstm_w_out[j])
        x = x + rmsnorm(y, mix_post_g[i])
        hn = rmsnorm(x, ffn_pre_g[i])
        y = sq_relu_mlp(hn, mlp_w1[i], mlp_w2[i])
        x = x + rmsnorm(y, ffn_post_g[i])
    return x
```

```python
import functools

import jax
import jax.numpy as jnp
from jax import lax
from jax.experimental import pallas as pl
from jax.experimental.pallas import tpu as pltpu

F32 = jnp.float32
BF16 = jnp.bfloat16

RMS_EPS = 1e-6
POOL_WINDOWS = (2, 4, 8, 16)
N_HEADS = 8
CONV_WIDTH = 4

LANES = 128
SUBLANES = 8
VMEM_LIMIT_BYTES = 56 * 1024 * 1024

POOL_HALO = 16
CONV_HALO = SUBLANES
MLSTM_CHUNK = 128


def _rms(x, g):
    ms = jnp.mean(x * x, axis=-1, keepdims=True)
    return x * lax.rsqrt(ms + RMS_EPS) * g


def _resident(shape):
    nd = len(shape)
    return pl.BlockSpec(shape, lambda *_: (0,) * nd, pipeline_mode=pl.Buffered(1))


def _mlp_kernel(x_ref, gpre_ref, gpost_ref, w1_ref, w2_ref, o_ref, *, fc):
    x = x_ref[...]
    hn = _rms(x, gpre_ref[...]).astype(BF16)
    d_ff = w1_ref.shape[1]
    acc = None
    for c in range(d_ff // fc):
        h = jnp.dot(hn, w1_ref[:, c * fc:(c + 1) * fc], preferred_element_type=F32)
        a = jnp.maximum(h, 0.0)
        a = (a * a).astype(BF16)
        p = jnp.dot(a, w2_ref[c * fc:(c + 1) * fc, :], preferred_element_type=F32)
        acc = p if acc is None else acc + p
    o_ref[...] = x + _rms(acc, gpost_ref[...])


def _mlp_layer(x2, gpre, gpost, w1, w2, *, tm=512, fc=1024):
    t, d = x2.shape
    d_ff = w1.shape[1]
    return pl.pallas_call(
        functools.partial(_mlp_kernel, fc=fc),
        grid=(t // tm,),
        in_specs=[
            pl.BlockSpec((tm, d), lambda i: (i, 0)),
            _resident((1, d)),
            _resident((1, d)),
            _resident((d, d_ff)),
            _resident((d_ff, d)),
        ],
        out_specs=pl.BlockSpec((tm, d), lambda i: (i, 0)),
        out_shape=jax.ShapeDtypeStruct((t, d), F32),
        compiler_params=pltpu.CompilerParams(
            dimension_semantics=("arbitrary",), vmem_limit_bytes=VMEM_LIMIT_BYTES),
        name="mlp_sublayer",
    )(x2, gpre, gpost, w1, w2)


def _pool_kernel(x_ref, gpre_ref, gpost_ref, w_ref, b_ref, scale_ref, o_ref, halo_ref, *, ts):
    j = pl.program_id(1)

    @pl.when(j == 0)
    def _():
        halo_ref[...] = jnp.zeros_like(halo_ref)

    x = x_ref[0]
    d = x.shape[-1]
    cg = d // len(POOL_WINDOWS)
    hn = _rms(x, gpre_ref[...])
    ext = jnp.concatenate([halo_ref[...], hn], axis=0)
    halo_ref[...] = hn[ts - POOL_HALO:, :]
    t = (j * ts + lax.broadcasted_iota(jnp.int32, (ts, 1), 0)).astype(F32)
    parts = []
    for g, w in enumerate(POOL_WINDOWS):
        e = ext[:, g * cg:(g + 1) * cg]
        sh = 1
        while sh < w:
            e = e + pltpu.roll(e, sh, axis=0)
            sh *= 2
        inv_cnt = 1.0 / jnp.minimum(t + 1.0, float(w))
        pooled = e[POOL_HALO:, :] * inv_cnt - hn[:, g * cg:(g + 1) * cg]
        parts.append(jnp.dot(pooled.astype(BF16), w_ref[g], preferred_element_type=F32))
    y = jnp.concatenate(parts, axis=-1)
    y = (y + b_ref[...]) * scale_ref[...]
    o_ref[0] = x + _rms(y, gpost_ref[...])


def _pool_layer(x, gpre, gpost, w, b, scale, *, ts=256):
    bsz, s, d = x.shape
    return pl.pallas_call(
        functools.partial(_pool_kernel, ts=ts),
        grid=(bsz, s // ts),
        in_specs=[
            pl.BlockSpec((1, ts, d), lambda i, j: (i, j, 0)),
            _resident((1, d)),
            _resident((1, d)),
            _resident(w.shape),
            _resident((1, d)),
            _resident((1, d)),
        ],
        out_specs=pl.BlockSpec((1, ts, d), lambda i, j: (i, j, 0)),
        out_shape=jax.ShapeDtypeStruct(x.shape, F32),
        scratch_shapes=[pltpu.VMEM((POOL_HALO, d), F32)],
        compiler_params=pltpu.CompilerParams(
            dimension_semantics=("arbitrary", "arbitrary"), vmem_limit_bytes=VMEM_LIMIT_BYTES),
        name="pool_sublayer",
    )(x, gpre, gpost, w, b, scale)


def _scan_rows(x, op, fill):
    n = x.shape[0]
    row = lax.broadcasted_iota(jnp.int32, x.shape, 0)
    sh = 1
    while sh < n:
        x = op(x, jnp.where(row >= sh, pltpu.roll(x, sh, axis=0), fill))
        sh *= 2
    return x


def _log_sigmoid(x):
    return -(jnp.maximum(-x, 0.0) + jnp.log1p(jnp.exp(-jnp.abs(x))))


def _mlstm_kernel(x_ref, gpre_ref, gpost_ref, w_ref, wg_ref, gb_ref, cw_ref, cb_ref,
                  hnw_ref, wout_ref, o_ref,
                  halo_ref, c_ref, n_ref, m_ref, q_ref, k_ref, v_ref, h_ref, *, ts, chunk):
    j = pl.program_id(1)

    @pl.when(j == 0)
    def _():
        halo_ref[...] = jnp.zeros_like(halo_ref)
        c_ref[...] = jnp.zeros_like(c_ref)
        n_ref[...] = jnp.zeros_like(n_ref)
        m_ref[...] = jnp.zeros_like(m_ref)

    x = x_ref[0]
    d = x.shape[-1]
    dh = d // N_HEADS
    hn = _rms(x, gpre_ref[...]).astype(BF16)

    for part, dst in ((0, q_ref), (1, k_ref)):
        cols = slice(part * d, (part + 1) * d)
        pre = jnp.dot(hn, w_ref[:, cols], preferred_element_type=F32)
        ext = jnp.concatenate([halo_ref[:, cols], pre], axis=0)
        halo_ref[:, cols] = pre[ts - CONV_HALO:, :]
        conv = cb_ref[:, cols]
        for tap in range(CONV_WIDTH):
            back = CONV_WIDTH - 1 - tap
            shifted = ext if back == 0 else pltpu.roll(ext, back, axis=0)
            conv = conv + cw_ref[tap:tap + 1, cols] * shifted[CONV_HALO:, :]
        act = conv * jax.nn.sigmoid(conv)
        if part == 0:
            dst[...] = act.astype(BF16)
        else:
            dst[...] = act * (dh ** -0.5)
    v_ref[...] = jnp.dot(hn, w_ref[:, 2 * d:3 * d], preferred_element_type=F32).astype(BF16)
    o_pre = jnp.dot(hn, w_ref[:, 3 * d:4 * d], preferred_element_type=F32)

    gates = jnp.dot(hn, wg_ref[...], preferred_element_type=F32) + gb_ref[...]
    ig_all = gates[:, :LANES]
    lf_all = _log_sigmoid(gates[:, LANES:])

    causal = (lax.broadcasted_iota(jnp.int32, (chunk, chunk), 0)
              >= lax.broadcasted_iota(jnp.int32, (chunk, chunk), 1))

    for c in range(ts // chunk):
        rows = slice(c * chunk, (c + 1) * chunk)
        ig = ig_all[rows, :]
        b = _scan_rows(lf_all[rows, :], jnp.add, 0.0)
        r = ig - b
        pm = _scan_rows(r, jnp.maximum, -jnp.inf)
        m_prev = m_ref[...]
        u = jnp.maximum(pm, m_prev)
        inter = jnp.exp(m_prev - u)
        inv_floor = jnp.exp(-(b + u))
        g = b[chunk - 1:chunk, :]
        a = g + r
        m_new = jnp.maximum(g + m_prev, jnp.max(a, axis=0, keepdims=True))
        decay = jnp.exp(g + m_prev - m_new)
        wts = jnp.exp(a - m_new)
        m_ref[...] = m_new
        r_t = r.T

        for h in range(N_HEADS):
            cs = slice(h * dh, (h + 1) * dh)
            hl = slice(h, h + 1)
            qh = q_ref[rows, cs]
            kf = k_ref[rows, cs]
            kh = kf.astype(BF16)
            vh = v_ref[rows, cs]
            s = lax.dot_general(qh, kh, (((1,), (1,)), ((), ())), preferred_element_type=F32)
            arg = jnp.where(causal, r_t[hl, :] - u[:, hl], -jnp.inf)
            s = s * jnp.exp(arg)
            c_old = c_ref[h]
            n_old = n_ref[hl, :]
            inter_h = inter[:, hl]
            num = (jnp.dot(s.astype(BF16), vh, preferred_element_type=F32)
                   + inter_h * jnp.dot(qh, c_old.astype(BF16), preferred_element_type=F32))
            qn = jnp.sum(qh.astype(F32) * n_old.astype(BF16).astype(F32), axis=-1, keepdims=True)
            den = jnp.sum(s, axis=-1, keepdims=True) + inter_h * qn
            hd = num * (1.0 / jnp.maximum(jnp.abs(den), inv_floor[:, hl]))
            hs = hd * lax.rsqrt(jnp.mean(hd * hd, axis=-1, keepdims=True) + RMS_EPS)
            h_ref[rows, cs] = hs
            kw = kf * wts[:, hl]
            decay_h = decay[:, hl]
            c_ref[h] = decay_h * c_old + lax.dot_general(
                kw.astype(BF16), vh, (((0,), (0,)), ((), ())), preferred_element_type=F32)
            n_ref[hl, :] = decay_h * n_old + jnp.sum(kw, axis=0, keepdims=True)

    gated = jax.nn.sigmoid(o_pre) * (h_ref[...] * hnw_ref[...])
    y = jnp.dot(gated.astype(BF16), wout_ref[...], preferred_element_type=F32)
    o_ref[0] = x + _rms(y, gpost_ref[...])


def _mlstm_layer(x, gpre, gpost, w_main, w_gate, gate_bias, conv_w, conv_b, hnw, w_out,
                 *, ts=256, chunk=MLSTM_CHUNK):
    bsz, s, d = x.shape
    dh = d // N_HEADS
    return pl.pallas_call(
        functools.partial(_mlstm_kernel, ts=ts, chunk=chunk),
        grid=(bsz, s // ts),
        in_specs=[
            pl.BlockSpec((1, ts, d), lambda i, j: (i, j, 0)),
            _resident((1, d)),
            _resident((1, d)),
            _resident(w_main.shape),
            _resident(w_gate.shape),
            _resident(gate_bias.shape),
            _resident(conv_w.shape),
            _resident(conv_b.shape),
            _resident((1, d)),
            _resident(w_out.shape),
        ],
        out_specs=pl.BlockSpec((1, ts, d), lambda i, j: (i, j, 0)),
        out_shape=jax.ShapeDtypeStruct(x.shape, F32),
        scratch_shapes=[
            pltpu.VMEM((CONV_HALO, 2 * d), F32),
            pltpu.VMEM((N_HEADS, dh, dh), F32),
            pltpu.VMEM((N_HEADS, dh), F32),
            pltpu.VMEM((1, LANES), F32),
            pltpu.VMEM((ts, d), BF16),
            pltpu.VMEM((ts, d), F32),
            pltpu.VMEM((ts, d), BF16),
            pltpu.VMEM((ts, d), F32),
        ],
        compiler_params=pltpu.CompilerParams(
            dimension_semantics=("arbitrary", "arbitrary"), vmem_limit_bytes=VMEM_LIMIT_BYTES),
        name="mlstm_sublayer",
    )(x, gpre, gpost, w_main, w_gate, gate_bias, conv_w, conv_b, hnw, w_out)


def _pad_gates(w_in, i_bias, f_bias, d):
    h = N_HEADS
    wg = jnp.zeros((d, 2 * LANES), F32)
    wg = wg.at[:, :h].set(w_in[:, 4 * d:4 * d + h])
    wg = wg.at[:, LANES:LANES + h].set(w_in[:, 4 * d + h:])
    gb = jnp.zeros((1, 2 * LANES), F32)
    gb = gb.at[0, :h].set(i_bias)
    gb = gb.at[0, LANES:LANES + h].set(f_bias)
    return wg.astype(BF16), gb


def kernel(x, mix_pre_g, mix_post_g, ffn_pre_g, ffn_post_g, pool_w, pool_b, pool_scale,
           mlstm_w_in, mlstm_conv_w, mlstm_conv_b, mlstm_i_bias, mlstm_f_bias,
           mlstm_head_norm_w, mlstm_w_out, mlp_w1, mlp_w2):
    bsz, s, d = x.shape
    depth = mix_pre_g.shape[0]
    row = lambda a: a.reshape(1, -1)
    pool_w16 = pool_w.astype(BF16)
    w1_16 = mlp_w1.astype(BF16)
    w2_16 = mlp_w2.astype(BF16)
    w_out16 = mlstm_w_out.astype(BF16)
    for i in range(depth):
        j = i // 2
        if i % 2 == 0:
            x = _pool_layer(x, row(mix_pre_g[i]), row(mix_post_g[i]), pool_w16[j],
                            row(pool_b[j]), row(pool_scale[j]))
        else:
            w_in = mlstm_w_in[j]
            w_gate, gate_bias = _pad_gates(w_in, mlstm_i_bias[j], mlstm_f_bias[j], d)
            x = _mlstm_layer(x, row(mix_pre_g[i]), row(mix_post_g[i]),
                             w_in[:, :4 * d].astype(BF16), w_gate, gate_bias,
                             mlstm_conv_w[j], row(mlstm_conv_b[j]),
                             row(mlstm_head_norm_w[j]), w_out16[j])
        x2 = _mlp_layer(x.reshape(bsz * s, d), row(ffn_pre_g[i]), row(ffn_post_g[i]),
                        w1_16[i], w2_16[i])
        x = x2.reshape(bsz, s, d)
    return x
```

```python
import functools

import jax
import jax.numpy as jnp
from jax import lax
from jax.experimental import pallas as pl
from jax.experimental.pallas import tpu as pltpu

F32 = jnp.float32
BF16 = jnp.bfloat16

RMS_EPS = 1e-6
POOL_WINDOWS = (2, 4, 8, 16)
N_HEADS = 8
CONV_WIDTH = 4

LANES = 128
SUBLANES = 8
VMEM_LIMIT_BYTES = 56 * 1024 * 1024

POOL_HALO = 16
CONV_HALO = SUBLANES
MLSTM_CHUNK = 128


def _rms(x, g):
    ms = jnp.mean(x * x, axis=-1, keepdims=True)
    return x * lax.rsqrt(ms + RMS_EPS) * g


def _resident(shape):
    nd = len(shape)
    return pl.BlockSpec(shape, lambda *_: (0,) * nd, pipeline_mode=pl.Buffered(1))


def _mlp_kernel(x_ref, gpre_ref, gpost_ref, w1_ref, w2_ref, o_ref, *, fc):
    x = x_ref[...]
    hn = _rms(x, gpre_ref[...]).astype(BF16)
    d_ff = w1_ref.shape[1]
    acc = None
    for c in range(d_ff // fc):
        h = jnp.dot(hn, w1_ref[:, c * fc:(c + 1) * fc], preferred_element_type=F32)
        a = jnp.maximum(h, 0.0)
        a = (a * a).astype(BF16)
        p = jnp.dot(a, w2_ref[c * fc:(c + 1) * fc, :], preferred_element_type=F32)
        acc = p if acc is None else acc + p
    o_ref[...] = x + _rms(acc, gpost_ref[...])


def _mlp_layer(x2, gpre, gpost, w1, w2, *, tm=512, fc=1024):
    t, d = x2.shape
    d_ff = w1.shape[1]
    return pl.pallas_call(
        functools.partial(_mlp_kernel, fc=fc),
        grid=(t // tm,),
        in_specs=[
            pl.BlockSpec((tm, d), lambda i: (i, 0)),
            _resident((1, d)),
            _resident((1, d)),
            _resident((d, d_ff)),
            _resident((d_ff, d)),
        ],
        out_specs=pl.BlockSpec((tm, d), lambda i: (i, 0)),
        out_shape=jax.ShapeDtypeStruct((t, d), F32),
        compiler_params=pltpu.CompilerParams(
            dimension_semantics=("arbitrary",), vmem_limit_bytes=VMEM_LIMIT_BYTES),
        name="mlp_sublayer",
    )(x2, gpre, gpost, w1, w2)


def _pool_kernel(x_ref, gpre_ref, gpost_ref, w_ref, b_ref, scale_ref, o_ref, halo_ref, *, ts):
    j = pl.program_id(1)

    @pl.when(j == 0)
    def _():
        halo_ref[...] = jnp.zeros_like(halo_ref)

    x = x_ref[0]
    d = x.shape[-1]
    cg = d // len(POOL_WINDOWS)
    hn = _rms(x, gpre_ref[...])
    ext = jnp.concatenate([halo_ref[...], hn], axis=0)
    halo_ref[...] = hn[ts - POOL_HALO:, :]
    t = (j * ts + lax.broadcasted_iota(jnp.int32, (ts, 1), 0)).astype(F32)
    parts = []
    for g, w in enumerate(POOL_WINDOWS):
        e = ext[:, g * cg:(g + 1) * cg]
        sh = 1
        while sh < w:
            e = e + pltpu.roll(e, sh, axis=0)
            sh *= 2
        inv_cnt = 1.0 / jnp.minimum(t + 1.0, float(w))
        pooled = e[POOL_HALO:, :] * inv_cnt - hn[:, g * cg:(g + 1) * cg]
        parts.append(jnp.dot(pooled.astype(BF16), w_ref[g], preferred_element_type=F32))
    y = jnp.concatenate(parts, axis=-1)
    y = (y + b_ref[...]) * scale_ref[...]
    o_ref[0] = x + _rms(y, gpost_ref[...])


def _pool_layer(x, gpre, gpost, w, b, scale, *, ts=256):
    bsz, s, d = x.shape
    return pl.pallas_call(
        functools.partial(_pool_kernel, ts=ts),
        grid=(bsz, s // ts),
        in_specs=[
            pl.BlockSpec((1, ts, d), lambda i, j: (i, j, 0)),
            _resident((1, d)),
            _resident((1, d)),
            _resident(w.shape),
            _resident((1, d)),
            _resident((1, d)),
        ],
        out_specs=pl.BlockSpec((1, ts, d), lambda i, j: (i, j, 0)),
        out_shape=jax.ShapeDtypeStruct(x.shape, F32),
        scratch_shapes=[pltpu.VMEM((POOL_HALO, d), F32)],
        compiler_params=pltpu.CompilerParams(
            dimension_semantics=("arbitrary", "arbitrary"), vmem_limit_bytes=VMEM_LIMIT_BYTES),
        name="pool_sublayer",
    )(x, gpre, gpost, w, b, scale)


def _scan_rows(x, op, fill):
    n = x.shape[0]
    row = lax.broadcasted_iota(jnp.int32, x.shape, 0)
    sh = 1
    while sh < n:
        x = op(x, jnp.where(row >= sh, pltpu.roll(x, sh, axis=0), fill))
        sh *= 2
    return x


def _log_sigmoid(x):
    return -(jnp.maximum(-x, 0.0) + jnp.log1p(jnp.exp(-jnp.abs(x))))


def _project_block(hn, blk, w_ref, pre_ref, dst, *, ts, bw):
    _, _, v_ref, o_ref, _ = dst
    d = v_ref.shape[-1]
    part, off = divmod(blk * bw, d)
    cols = slice(blk * bw, (blk + 1) * bw)
    oc = slice(off, off + bw)
    if part < 2:
        pre_ref[:CONV_HALO, cols] = pre_ref[ts:, cols]
    pre = jnp.dot(hn, w_ref[:, cols], preferred_element_type=F32)
    if part < 2:
        pre_ref[CONV_HALO:, cols] = pre
    elif part == 2:
        v_ref[:, oc] = pre.astype(BF16)
    else:
        o_ref[:, oc] = pre


def _conv_block(blk, cw_ref, cb_ref, pre_ref, dst, *, bw):
    q_ref, k_ref = dst[:2]
    d = q_ref.shape[-1]
    dh = d // N_HEADS
    part, off = divmod(blk * bw, d)
    cols = slice(blk * bw, (blk + 1) * bw)
    oc = slice(off, off + bw)
    ext = pre_ref[:, cols]
    conv = cb_ref[:, cols]
    for tap in range(CONV_WIDTH):
        back = CONV_WIDTH - 1 - tap
        shifted = ext if back == 0 else pltpu.roll(ext, back, axis=0)
        conv = conv + cw_ref[tap:tap + 1, cols] * shifted[CONV_HALO:, :]
    act = conv * jax.nn.sigmoid(conv)
    if part == 0:
        q_ref[:, oc] = act.astype(BF16)
    else:
        k_ref[:, oc] = act * (dh ** -0.5)


def _chunk_gates(g_ref, m_ref, rows, chunk):
    ig = g_ref[rows, :LANES]
    b = _scan_rows(_log_sigmoid(g_ref[rows, LANES:]), jnp.add, 0.0)
    r = ig - b
    pm = _scan_rows(r, jnp.maximum, -jnp.inf)
    m_prev = m_ref[...]
    u = jnp.maximum(pm, m_prev)
    g = b[chunk - 1:chunk, :]
    a = g + r
    m_new = jnp.maximum(g + m_prev, jnp.max(a, axis=0, keepdims=True))
    m_ref[...] = m_new
    return dict(
        u=u,
        inter=jnp.exp(m_prev - u),
        inv_floor=jnp.exp(-(b + u)),
        decay=jnp.exp(g + m_prev - m_new),
        wts=jnp.exp(a - m_new),
        r_t=r.T,
    )


def _head_scores(h, rows, gt, causal, src):
    q_ref, k_ref = src[:2]
    dh = q_ref.shape[-1] // N_HEADS
    cs = slice(h * dh, (h + 1) * dh)
    hl = slice(h, h + 1)
    s = lax.dot_general(q_ref[rows, cs], k_ref[rows, cs].astype(BF16), (((1,), (1,)), ((), ())),
                        preferred_element_type=F32)
    arg = jnp.where(causal, gt["r_t"][hl, :] - gt["u"][:, hl], -jnp.inf)
    return s * jnp.exp(arg)


def _head_finish(h, rows, gt, s, src, c_ref, n_ref, h_ref):
    q_ref, k_ref, v_ref, _, _ = src
    dh = q_ref.shape[-1] // N_HEADS
    cs = slice(h * dh, (h + 1) * dh)
    hl = slice(h, h + 1)
    qh = q_ref[rows, cs]
    kf = k_ref[rows, cs]
    vh = v_ref[rows, cs]
    c_old = c_ref[h]
    n_old = n_ref[hl, :]
    inter_h = gt["inter"][:, hl]
    num = (jnp.dot(s.astype(BF16), vh, preferred_element_type=F32)
           + inter_h * jnp.dot(qh, c_old.astype(BF16), preferred_element_type=F32))
    qn = jnp.sum(qh.astype(F32) * n_old.astype(BF16).astype(F32), axis=-1, keepdims=True)
    den = jnp.sum(s, axis=-1, keepdims=True) + inter_h * qn
    hd = num * (1.0 / jnp.maximum(jnp.abs(den), gt["inv_floor"][:, hl]))
    hs = hd * lax.rsqrt(jnp.mean(hd * hd, axis=-1, keepdims=True) + RMS_EPS)
    h_ref[rows, cs] = hs
    kw = kf * gt["wts"][:, hl]
    decay_h = gt["decay"][:, hl]
    c_ref[h] = decay_h * c_old + lax.dot_general(
        kw.astype(BF16), vh, (((0,), (0,)), ((), ())), preferred_element_type=F32)
    n_ref[hl, :] = decay_h * n_old + jnp.sum(kw, axis=0, keepdims=True)


def _mlstm_step(xa_ref, xb_ref, gpre_ref, gpost_ref, w_ref, wg_ref, gb_ref, cw_ref, cb_ref,
                hnw_ref, wout_ref, out_ref, halo_ref, c_ref, n_ref, m_ref, h_ref, dst, src,
                *, ts, chunk, bw):
    d = xa_ref.shape[-1]
    hn = _rms(xa_ref[...], gpre_ref[...]).astype(BF16)
    causal = (lax.broadcasted_iota(jnp.int32, (chunk, chunk), 0)
              >= lax.broadcasted_iota(jnp.int32, (chunk, chunk), 1))
    n_blocks = 4 * d // bw
    n_conv = 2 * d // bw
    units = [(c, h) for c in range(ts // chunk) for h in range(N_HEADS)]
    rows_of = lambda c: slice(c * chunk, (c + 1) * chunk)
    gates = {c: _chunk_gates(src[4], m_ref, rows_of(c), chunk) for c in range(ts // chunk)}
    scores = lambda u: _head_scores(units[u][1], rows_of(units[u][0]), gates[units[u][0]], causal, src)

    _project_block(hn, 0, w_ref, halo_ref, dst, ts=ts, bw=bw)
    s_next = scores(0)
    for i in range(max(n_blocks, len(units))):
        if i + 1 < n_blocks:
            _project_block(hn, i + 1, w_ref, halo_ref, dst, ts=ts, bw=bw)
        if i < n_conv:
            _conv_block(i, cw_ref, cb_ref, halo_ref, dst, bw=bw)
        if i < len(units):
            c, h = units[i]
            s_cur = s_next
            if i + 1 < len(units):
                s_next = scores(i + 1)
            _head_finish(h, rows_of(c), gates[c], s_cur, src, c_ref, n_ref, h_ref)
    dst[4][...] = jnp.dot(hn, wg_ref[...], preferred_element_type=F32) + gb_ref[...]

    gated = jax.nn.sigmoid(src[3][...]) * (h_ref[...] * hnw_ref[...])
    y = jnp.dot(gated.astype(BF16), wout_ref[...], preferred_element_type=F32)
    out_ref[...] = xb_ref[...] + _rms(y, gpost_ref[...])


def _mlstm_kernel(xa_ref, xb_ref, gpre_ref, gpost_ref, w_ref, wg_ref, gb_ref, cw_ref, cb_ref,
                  hnw_ref, wout_ref, out_ref,
                  halo_ref, c_ref, n_ref, m_ref, h_ref, *bufs, ts, chunk, bw, tiles_per_seq):
    step = pl.program_id(0)
    buf_a, buf_b = bufs[:5], bufs[5:]

    @pl.when(step == 0)
    def _():
        for ref in buf_b:
            ref[...] = jnp.zeros_like(ref)

    @pl.when(step % tiles_per_seq == 0)
    def _():
        halo_ref[ts:, :] = jnp.zeros((CONV_HALO, halo_ref.shape[1]), F32)

    @pl.when((step + tiles_per_seq - 1) % tiles_per_seq == 0)
    def _():
        c_ref[...] = jnp.zeros_like(c_ref)
        n_ref[...] = jnp.zeros_like(n_ref)
        m_ref[...] = jnp.zeros_like(m_ref)

    def run(dst, src):
        _mlstm_step(xa_ref, xb_ref, gpre_ref, gpost_ref, w_ref, wg_ref, gb_ref, cw_ref, cb_ref,
                    hnw_ref, wout_ref, out_ref, halo_ref, c_ref, n_ref, m_ref, h_ref, dst, src,
                    ts=ts, chunk=chunk, bw=bw)

    @pl.when(step % 2 == 0)
    def _():
        run(buf_a, buf_b)

    @pl.when(step % 2 == 1)
    def _():
        run(buf_b, buf_a)


def _mlstm_layer(x2, gpre, gpost, w_main, w_gate, gate_bias, conv_w, conv_b, hnw, w_out,
                 *, seq_len, ts=256, chunk=MLSTM_CHUNK, bw=256):
    t, d = x2.shape
    dh = d // N_HEADS
    n_tiles = t // ts
    buf_set = [
        pltpu.VMEM((ts, d), BF16),
        pltpu.VMEM((ts, d), F32),
        pltpu.VMEM((ts, d), BF16),
        pltpu.VMEM((ts, d), F32),
        pltpu.VMEM((ts, 2 * LANES), F32),
    ]
    return pl.pallas_call(
        functools.partial(_mlstm_kernel, ts=ts, chunk=chunk, bw=bw, tiles_per_seq=seq_len // ts),
        grid=(n_tiles + 1,),
        in_specs=[
            pl.BlockSpec((ts, d), lambda i: (jnp.minimum(i, n_tiles - 1), 0)),
            pl.BlockSpec((ts, d), lambda i: (jnp.maximum(i - 1, 0), 0)),
            _resident((1, d)),
            _resident((1, d)),
            _resident(w_main.shape),
            _resident(w_gate.shape),
            _resident(gate_bias.shape),
            _resident(conv_w.shape),
            _resident(conv_b.shape),
            _resident((1, d)),
            _resident(w_out.shape),
        ],
        out_specs=pl.BlockSpec((ts, d), lambda i: (jnp.maximum(i - 1, 0), 0)),
        out_shape=jax.ShapeDtypeStruct((t, d), F32),
        scratch_shapes=[
            pltpu.VMEM((CONV_HALO + ts, 2 * d), F32),
            pltpu.VMEM((N_HEADS, dh, dh), F32),
            pltpu.VMEM((N_HEADS, dh), F32),
            pltpu.VMEM((1, LANES), F32),
            pltpu.VMEM((ts, d), F32),
        ] + buf_set + buf_set,
        compiler_params=pltpu.CompilerParams(
            dimension_semantics=("arbitrary",), vmem_limit_bytes=VMEM_LIMIT_BYTES),
        name="mlstm_sublayer",
    )(x2, x2, gpre, gpost, w_main, w_gate, gate_bias, conv_w, conv_b, hnw, w_out)


def _pad_gates(w_in, i_bias, f_bias, d):
    h = N_HEADS
    wg = jnp.zeros((d, 2 * LANES), F32)
    wg = wg.at[:, :h].set(w_in[:, 4 * d:4 * d + h])
    wg = wg.at[:, LANES:LANES + h].set(w_in[:, 4 * d + h:])
    gb = jnp.zeros((1, 2 * LANES), F32)
    gb = gb.at[0, :h].set(i_bias)
    gb = gb.at[0, LANES:LANES + h].set(f_bias)
    return wg.astype(BF16), gb


def kernel(x, mix_pre_g, mix_post_g, ffn_pre_g, ffn_post_g, pool_w, pool_b, pool_scale,
           mlstm_w_in, mlstm_conv_w, mlstm_conv_b, mlstm_i_bias, mlstm_f_bias,
           mlstm_head_norm_w, mlstm_w_out, mlp_w1, mlp_w2):
    bsz, s, d = x.shape
    depth = mix_pre_g.shape[0]
    row = lambda a: a.reshape(1, -1)
    pool_w16 = pool_w.astype(BF16)
    w1_16 = mlp_w1.astype(BF16)
    w2_16 = mlp_w2.astype(BF16)
    w_out16 = mlstm_w_out.astype(BF16)
    for i in range(depth):
        j = i // 2
        if i % 2 == 0:
            x = _pool_layer(x, row(mix_pre_g[i]), row(mix_post_g[i]), pool_w16[j],
                            row(pool_b[j]), row(pool_scale[j]))
        else:
            w_in = mlstm_w_in[j]
            w_gate, gate_bias = _pad_gates(w_in, mlstm_i_bias[j], mlstm_f_bias[j], d)
            x2 = _mlstm_layer(x.reshape(bsz * s, d), row(mix_pre_g[i]), row(mix_post_g[i]),
                              w_in[:, :4 * d].astype(BF16), w_gate, gate_bias,
                              mlstm_conv_w[j], row(mlstm_conv_b[j]),
                              row(mlstm_head_norm_w[j]), w_out16[j], seq_len=s)
            x = x2.reshape(bsz, s, d)
        x2 = _mlp_layer(x.reshape(bsz * s, d), row(ffn_pre_g[i]), row(ffn_post_g[i]),
                        w1_16[i], w2_16[i])
        x = x2.reshape(bsz, s, d)
    return x
```

```python
import functools

import jax
import jax.numpy as jnp
from jax import lax
from jax.experimental import pallas as pl
from jax.experimental.pallas import tpu as pltpu

F32 = jnp.float32
BF16 = jnp.bfloat16

RMS_EPS = 1e-6
POOL_WINDOWS = (2, 4, 8, 16)
N_HEADS = 8
CONV_WIDTH = 4

LANES = 128
SUBLANES = 8
VMEM_LIMIT_BYTES = 60 * 1024 * 1024

POOL_HALO = 16
CONV_HALO = SUBLANES
MLSTM_CHUNK = 128
ROW_TILE = 512
MLP_CHUNK = 512
PROJ_BLOCK = 256


def _rms(x, g):
    ms = jnp.mean(x * x, axis=-1, keepdims=True)
    return x * lax.rsqrt(ms + RMS_EPS) * g


def _resident(shape):
    nd = len(shape)
    return pl.BlockSpec(shape, lambda *_: (0,) * nd, pipeline_mode=pl.Buffered(1))


def _interleave(*streams):
    n = max(len(s) for s in streams)
    pos = [0] * len(streams)
    for i in range(n):
        for k, s in enumerate(streams):
            upto = -(-(i + 1) * len(s) // n)
            while pos[k] < upto:
                s[pos[k]]()
                pos[k] += 1


def _mlp_pieces(load_x, gpre_ref, gpost_ref, w1_ref, w2_ref, out_ref):
    st = {}
    d_ff = w1_ref.shape[1]

    def head():
        st["hn"] = _rms(load_x(), gpre_ref[...]).astype(BF16)
        st["acc"] = None

    def chunk(c):
        def run():
            cols = slice(c * MLP_CHUNK, (c + 1) * MLP_CHUNK)
            h = jnp.dot(st["hn"], w1_ref[:, cols], preferred_element_type=F32)
            a = jnp.maximum(h, 0.0)
            a = (a * a).astype(BF16)
            p = jnp.dot(a, w2_ref[cols, :], preferred_element_type=F32)
            st["acc"] = p if st["acc"] is None else st["acc"] + p
        return run

    def tail():
        out_ref[...] = load_x() + _rms(st["acc"], gpost_ref[...])

    return [head] + [chunk(c) for c in range(d_ff // MLP_CHUNK)] + [tail]


def _skew_slots(step, x1_ref):
    new = step % 2

    @pl.when(step == 0)
    def _():
        x1_ref[1] = jnp.zeros(x1_ref.shape[1:], F32)

    return 1 - new, new


def _pool_pieces(x_ref, gpre_ref, gpost_ref, w_ref, b_ref, scale_ref, halo_ref, store, first_row):
    st = {"parts": []}
    ts, d = x_ref.shape
    cg = d // len(POOL_WINDOWS)

    def head():
        x = x_ref[...]
        hn = _rms(x, gpre_ref[...])
        st["hn"] = hn
        st["ext"] = jnp.concatenate([halo_ref[...], hn], axis=0)
        halo_ref[...] = hn[ts - POOL_HALO:, :]
        st["t"] = (first_row + lax.broadcasted_iota(jnp.int32, (ts, 1), 0)).astype(F32)

    def group(g, w):
        def run():
            e = st["ext"][:, g * cg:(g + 1) * cg]
            sh = 1
            while sh < w:
                e = e + pltpu.roll(e, sh, axis=0)
                sh *= 2
            inv_cnt = 1.0 / jnp.minimum(st["t"] + 1.0, float(w))
            pooled = e[POOL_HALO:, :] * inv_cnt - st["hn"][:, g * cg:(g + 1) * cg]
            st["parts"].append(jnp.dot(pooled.astype(BF16), w_ref[g], preferred_element_type=F32))
        return run

    def tail():
        y = jnp.concatenate(st["parts"], axis=-1)
        y = (y + b_ref[...]) * scale_ref[...]
        store(x_ref[...] + _rms(y, gpost_ref[...]))

    return [head] + [group(g, w) for g, w in enumerate(POOL_WINDOWS)] + [tail]


def _pool_mlp_kernel(x_ref, gpre_ref, gpost_ref, w_ref, b_ref, scale_ref,
                     fpre_ref, fpost_ref, w1_ref, w2_ref, out_ref, halo_ref, x1_ref,
                     *, tiles_per_seq):
    step = pl.program_id(0)
    ts = x_ref.shape[0]
    old, new = _skew_slots(step, x1_ref)

    @pl.when(step % tiles_per_seq == 0)
    def _():
        halo_ref[...] = jnp.zeros_like(halo_ref)

    def store(v):
        x1_ref[new] = v

    mlp = _mlp_pieces(lambda: x1_ref[old], fpre_ref, fpost_ref, w1_ref, w2_ref, out_ref)
    pool = _pool_pieces(x_ref, gpre_ref, gpost_ref, w_ref, b_ref, scale_ref, halo_ref, store,
                        (step % tiles_per_seq) * ts)
    _interleave(mlp, pool)


def _pool_mlp_layer(x2, gpre, gpost, w, b, scale, fpre, fpost, w1, w2, *, seq_len, ts=ROW_TILE):
    t, d = x2.shape
    n_tiles = t // ts
    return pl.pallas_call(
        functools.partial(_pool_mlp_kernel, tiles_per_seq=seq_len // ts),
        grid=(n_tiles + 1,),
        in_specs=[
            pl.BlockSpec((ts, d), lambda i: (jnp.minimum(i, n_tiles - 1), 0)),
            _resident((1, d)), _resident((1, d)), _resident(w.shape),
            _resident((1, d)), _resident((1, d)),
            _resident((1, d)), _resident((1, d)), _resident(w1.shape), _resident(w2.shape),
        ],
        out_specs=pl.BlockSpec((ts, d), lambda i: (jnp.maximum(i - 1, 0), 0)),
        out_shape=jax.ShapeDtypeStruct((t, d), F32),
        scratch_shapes=[
            pltpu.VMEM((POOL_HALO, d), F32),
            pltpu.VMEM((2, ts, d), F32),
        ],
        compiler_params=pltpu.CompilerParams(
            dimension_semantics=("arbitrary",), vmem_limit_bytes=VMEM_LIMIT_BYTES),
        name="pool_mlp_layer",
    )(x2, gpre, gpost, w, b, scale, fpre, fpost, w1, w2)


def _project_block(hn, blk, w_ref, pre_ref, v_ref, o_ref, *, ts):
    d = v_ref.shape[-1]
    part, off = divmod(blk * PROJ_BLOCK, d)
    cols = slice(blk * PROJ_BLOCK, (blk + 1) * PROJ_BLOCK)
    oc = slice(off, off + PROJ_BLOCK)
    if part < 2:
        pre_ref[:CONV_HALO, cols] = pre_ref[ts:, cols]
    pre = jnp.dot(hn, w_ref[:, cols], preferred_element_type=F32)
    if part < 2:
        pre_ref[CONV_HALO:, cols] = pre
    elif part == 2:
        v_ref[:, oc] = pre.astype(BF16)
    else:
        o_ref[:, oc] = pre


def _conv_block(blk, cw_ref, cb_ref, pre_ref, q_ref, k_ref):
    d = q_ref.shape[-1]
    dh = d // N_HEADS
    part, off = divmod(blk * PROJ_BLOCK, d)
    cols = slice(blk * PROJ_BLOCK, (blk + 1) * PROJ_BLOCK)
    oc = slice(off, off + PROJ_BLOCK)
    ext = pre_ref[:, cols]
    conv = cb_ref[:, cols]
    for tap in range(CONV_WIDTH):
        back = CONV_WIDTH - 1 - tap
        shifted = ext if back == 0 else pltpu.roll(ext, back, axis=0)
        conv = conv + cw_ref[tap:tap + 1, cols] * shifted[CONV_HALO:, :]
    act = conv * jax.nn.sigmoid(conv)
    if part == 0:
        q_ref[:, oc] = act.astype(BF16)
    else:
        k_ref[:, oc] = act * (dh ** -0.5)


def _mlstm_proj_kernel(x_ref, gpre_ref, w_ref, wg_ref, gb_ref, cw_ref, cb_ref,
                       q_ref, k_ref, v_ref, o_ref, g_ref, pre_ref, *, tiles_per_seq):
    step = pl.program_id(0)
    ts, d = x_ref.shape

    @pl.when(step % tiles_per_seq == 0)
    def _():
        pre_ref[ts:, :] = jnp.zeros((CONV_HALO, pre_ref.shape[1]), F32)

    hn = _rms(x_ref[...], gpre_ref[...]).astype(BF16)
    n_blocks = 4 * d // PROJ_BLOCK
    n_conv = 2 * d // PROJ_BLOCK
    order = [b for pair in zip(range(n_conv), range(n_conv, n_blocks)) for b in pair]
    _project_block(hn, order[0], w_ref, pre_ref, v_ref, o_ref, ts=ts)
    for i in range(n_blocks):
        if i + 1 < n_blocks:
            _project_block(hn, order[i + 1], w_ref, pre_ref, v_ref, o_ref, ts=ts)
        if order[i] < n_conv:
            _conv_block(order[i], cw_ref, cb_ref, pre_ref, q_ref, k_ref)
    g_ref[...] = jnp.dot(hn, wg_ref[...], preferred_element_type=F32) + gb_ref[...]


def _mlstm_proj_layer(x2, gpre, w_main, w_gate, gate_bias, conv_w, conv_b, *, seq_len, ts=ROW_TILE):
    t, d = x2.shape
    tile = lambda width: pl.BlockSpec((ts, width), lambda i: (i, 0))
    return pl.pallas_call(
        functools.partial(_mlstm_proj_kernel, tiles_per_seq=seq_len // ts),
        grid=(t // ts,),
        in_specs=[
            tile(d), _resident((1, d)), _resident(w_main.shape), _resident(w_gate.shape),
            _resident(gate_bias.shape), _resident(conv_w.shape), _resident(conv_b.shape),
        ],
        out_specs=[tile(d), tile(d), tile(d), tile(d), tile(2 * LANES)],
        out_shape=[
            jax.ShapeDtypeStruct((t, d), BF16),
            jax.ShapeDtypeStruct((t, d), F32),
            jax.ShapeDtypeStruct((t, d), BF16),
            jax.ShapeDtypeStruct((t, d), F32),
            jax.ShapeDtypeStruct((t, 2 * LANES), F32),
        ],
        scratch_shapes=[pltpu.VMEM((CONV_HALO + ts, 2 * d), F32)],
        compiler_params=pltpu.CompilerParams(
            dimension_semantics=("arbitrary",), vmem_limit_bytes=VMEM_LIMIT_BYTES),
        name="mlstm_proj_layer",
    )(x2, gpre, w_main, w_gate, gate_bias, conv_w, conv_b)


def _scan_rows(x, op, fill):
    n = x.shape[0]
    row = lax.broadcasted_iota(jnp.int32, x.shape, 0)
    sh = 1
    while sh < n:
        x = op(x, jnp.where(row >= sh, pltpu.roll(x, sh, axis=0), fill))
        sh *= 2
    return x


def _log_sigmoid(x):
    return -(jnp.maximum(-x, 0.0) + jnp.log1p(jnp.exp(-jnp.abs(x))))


def _chunk_gates(g_ref, m_ref, rows):
    chunk = rows.stop - rows.start
    ig = g_ref[rows, :LANES]
    b = _scan_rows(_log_sigmoid(g_ref[rows, LANES:]), jnp.add, 0.0)
    r = ig - b
    pm = _scan_rows(r, jnp.maximum, -jnp.inf)
    m_prev = m_ref[...]
    u = jnp.maximum(pm, m_prev)
    g = b[chunk - 1:chunk, :]
    a = g + r
    m_new = jnp.maximum(g + m_prev, jnp.max(a, axis=0, keepdims=True))
    m_ref[...] = m_new
    return dict(
        u=u,
        inter=jnp.exp(m_prev - u),
        inv_floor=jnp.exp(-(b + u)),
        decay=jnp.exp(g + m_prev - m_new),
        wts=jnp.exp(a - m_new),
        r_t=r.T,
    )


def _head_scores(h, rows, gt, causal, q_ref, k_ref):
    dh = q_ref.shape[-1] // N_HEADS
    cs = slice(h * dh, (h + 1) * dh)
    hl = slice(h, h + 1)
    s = lax.dot_general(q_ref[rows, cs], k_ref[rows, cs].astype(BF16), (((1,), (1,)), ((), ())),
                        preferred_element_type=F32)
    arg = jnp.where(causal, gt["r_t"][hl, :] - gt["u"][:, hl], -jnp.inf)
    return s * jnp.exp(arg)


def _head_finish(h, rows, gt, s, q_ref, k_ref, v_ref, c_ref, n_ref, h_ref):
    dh = q_ref.shape[-1] // N_HEADS
    cs = slice(h * dh, (h + 1) * dh)
    hl = slice(h, h + 1)
    qh = q_ref[rows, cs]
    kf = k_ref[rows, cs]
    vh = v_ref[rows, cs]
    c_old = c_ref[h]
    n_old = n_ref[hl, :]
    inter_h = gt["inter"][:, hl]
    num = (jnp.dot(s.astype(BF16), vh, preferred_element_type=F32)
           + inter_h * jnp.dot(qh, c_old.astype(BF16), preferred_element_type=F32))
    qn = jnp.sum(qh.astype(F32) * n_old.astype(BF16).astype(F32), axis=-1, keepdims=True)
    den = jnp.sum(s, axis=-1, keepdims=True) + inter_h * qn
    rinv = 1.0 / jnp.maximum(jnp.abs(den), gt["inv_floor"][:, hl])
    ms = jnp.mean(num * num, axis=-1, keepdims=True)
    h_ref[rows, cs] = num * (rinv * lax.rsqrt(rinv * rinv * ms + RMS_EPS))
    kw = kf * gt["wts"][:, hl]
    decay_h = gt["decay"][:, hl]
    c_ref[h] = decay_h * c_old + lax.dot_general(
        kw.astype(BF16), vh, (((0,), (0,)), ((), ())), preferred_element_type=F32)
    n_ref[hl, :] = decay_h * n_old + jnp.sum(kw, axis=0, keepdims=True)


def _recur_pieces(x_ref, q_ref, k_ref, v_ref, o_ref, g_ref, gpost_ref, hnw_ref, wout_ref,
                  c_ref, n_ref, m_ref, h_ref, store):
    ts = x_ref.shape[0]
    chunk = MLSTM_CHUNK
    causal = (lax.broadcasted_iota(jnp.int32, (chunk, chunk), 0)
              >= lax.broadcasted_iota(jnp.int32, (chunk, chunk), 1))
    units = [(c, h) for c in range(ts // chunk) for h in range(N_HEADS)]
    rows_of = lambda c: slice(c * chunk, (c + 1) * chunk)
    gates, scores = {}, {}

    def prepare(u):
        if u >= len(units):
            return
        c, h = units[u]
        if c not in gates:
            gates[c] = _chunk_gates(g_ref, m_ref, rows_of(c))
        scores[u] = _head_scores(h, rows_of(c), gates[c], causal, q_ref, k_ref)

    def unit(u):
        def run():
            prepare(u + 1)
            c, h = units[u]
            _head_finish(h, rows_of(c), gates[c], scores.pop(u), q_ref, k_ref, v_ref,
                         c_ref, n_ref, h_ref)
        return run

    def tail():
        gated = jax.nn.sigmoid(o_ref[...]) * (h_ref[...] * hnw_ref[...])
        y = jnp.dot(gated.astype(BF16), wout_ref[...], preferred_element_type=F32)
        store(x_ref[...] + _rms(y, gpost_ref[...]))

    return [lambda: prepare(0)] + [unit(u) for u in range(len(units))] + [tail]


def _mlstm_mlp_kernel(x_ref, q_ref, k_ref, v_ref, o_ref, g_ref, gpost_ref, hnw_ref, wout_ref,
                      fpre_ref, fpost_ref, w1_ref, w2_ref, out_ref,
                      c_ref, n_ref, m_ref, h_ref, x1_ref, *, tiles_per_seq):
    step = pl.program_id(0)
    old, new = _skew_slots(step, x1_ref)

    @pl.when(step % tiles_per_seq == 0)
    def _():
        c_ref[...] = jnp.zeros_like(c_ref)
        n_ref[...] = jnp.zeros_like(n_ref)
        m_ref[...] = jnp.zeros_like(m_ref)

    def store(v):
        x1_ref[new] = v

    mlp = _mlp_pieces(lambda: x1_ref[old], fpre_ref, fpost_ref, w1_ref, w2_ref, out_ref)
    rec = _recur_pieces(x_ref, q_ref, k_ref, v_ref, o_ref, g_ref, gpost_ref, hnw_ref, wout_ref,
                        c_ref, n_ref, m_ref, h_ref, store)
    _interleave(mlp, rec)


def _mlstm_mlp_layer(x2, q, k, v, o, g, gpost, hnw, w_out, fpre, fpost, w1, w2,
                     *, seq_len, ts=ROW_TILE):
    t, d = x2.shape
    dh = d // N_HEADS
    n_tiles = t // ts
    tile = lambda width: pl.BlockSpec((ts, width), lambda i: (jnp.minimum(i, n_tiles - 1), 0))
    return pl.pallas_call(
        functools.partial(_mlstm_mlp_kernel, tiles_per_seq=seq_len // ts),
        grid=(n_tiles + 1,),
        in_specs=[
            tile(d), tile(d), tile(d), tile(d), tile(d), tile(2 * LANES),
            _resident((1, d)), _resident((1, d)), _resident(w_out.shape),
            _resident((1, d)), _resident((1, d)), _resident(w1.shape), _resident(w2.shape),
        ],
        out_specs=pl.BlockSpec((ts, d), lambda i: (jnp.maximum(i - 1, 0), 0)),
        out_shape=jax.ShapeDtypeStruct((t, d), F32),
        scratch_shapes=[
            pltpu.VMEM((N_HEADS, dh, dh), F32),
            pltpu.VMEM((N_HEADS, dh), F32),
            pltpu.VMEM((1, LANES), F32),
            pltpu.VMEM((ts, d), F32),
            pltpu.VMEM((2, ts, d), F32),
        ],
        compiler_params=pltpu.CompilerParams(
            dimension_semantics=("arbitrary",), vmem_limit_bytes=VMEM_LIMIT_BYTES),
        name="mlstm_mlp_layer",
    )(x2, q, k, v, o, g, gpost, hnw, w_out, fpre, fpost, w1, w2)


def _pad_gates(w_in, i_bias, f_bias, d):
    h = N_HEADS
    wg = jnp.zeros((d, 2 * LANES), F32)
    wg = wg.at[:, :h].set(w_in[:, 4 * d:4 * d + h])
    wg = wg.at[:, LANES:LANES + h].set(w_in[:, 4 * d + h:])
    gb = jnp.zeros((1, 2 * LANES), F32)
    gb = gb.at[0, :h].set(i_bias)
    gb = gb.at[0, LANES:LANES + h].set(f_bias)
    return wg.astype(BF16), gb


def kernel(x, mix_pre_g, mix_post_g, ffn_pre_g, ffn_post_g, pool_w, pool_b, pool_scale,
           mlstm_w_in, mlstm_conv_w, mlstm_conv_b, mlstm_i_bias, mlstm_f_bias,
           mlstm_head_norm_w, mlstm_w_out, mlp_w1, mlp_w2):
    bsz, s, d = x.shape
    depth = mix_pre_g.shape[0]
    row = lambda a: a.reshape(1, -1)
    pool_w16 = pool_w.astype(BF16)
    w1_16 = mlp_w1.astype(BF16)
    w2_16 = mlp_w2.astype(BF16)
    w_out16 = mlstm_w_out.astype(BF16)
    x2 = x.reshape(bsz * s, d)
    for i in range(depth):
        j = i // 2
        mlp_args = (row(ffn_pre_g[i]), row(ffn_post_g[i]), w1_16[i], w2_16[i])
        if i % 2 == 0:
            x2 = _pool_mlp_layer(x2, row(mix_pre_g[i]), row(mix_post_g[i]), pool_w16[j],
                                 row(pool_b[j]), row(pool_scale[j]), *mlp_args, seq_len=s)
        else:
            w_in = mlstm_w_in[j]
            w_gate, gate_bias = _pad_gates(w_in, mlstm_i_bias[j], mlstm_f_bias[j], d)
            q, k, v, o, g = _mlstm_proj_layer(
                x2, row(mix_pre_g[i]), w_in[:, :4 * d].astype(BF16), w_gate, gate_bias,
                mlstm_conv_w[j], row(mlstm_conv_b[j]), seq_len=s)
            x2 = _mlstm_mlp_layer(x2, q, k, v, o, g, row(mix_post_g[i]),
                                  row(mlstm_head_norm_w[j]), w_out16[j], *mlp_args, seq_len=s)
    return x2.reshape(bsz, s, d)
```

```python
import functools

import jax
import jax.numpy as jnp
from jax import lax
from jax.experimental import pallas as pl
from jax.experimental.pallas import tpu as pltpu

F32 = jnp.float32
BF16 = jnp.bfloat16

RMS_EPS = 1e-6
POOL_WINDOWS = (2, 4, 8, 16)
N_HEADS = 8
CONV_WIDTH = 4

LANES = 128
SUBLANES = 8
VMEM_LIMIT_BYTES = 60 * 1024 * 1024

POOL_HALO = 16
CONV_HALO = SUBLANES
MLSTM_CHUNK = 128
ROW_TILE = 512
MLP_CHUNK = 512
PROJ_BLOCK = 256


def _rms(x, g):
    ms = jnp.mean(x * x, axis=-1, keepdims=True)
    return x * lax.rsqrt(ms + RMS_EPS) * g


def _resident(shape):
    nd = len(shape)
    return pl.BlockSpec(shape, lambda *_: (0,) * nd, pipeline_mode=pl.Buffered(1))


def _interleave(*streams):
    n = max(len(s) for s in streams)
    pos = [0] * len(streams)
    for i in range(n):
        for k, s in enumerate(streams):
            upto = -(-(i + 1) * len(s) // n)
            while pos[k] < upto:
                s[pos[k]]()
                pos[k] += 1


def _mlp_pieces(load_x, gpre_ref, gpost_ref, w1_ref, w2_ref, out_ref):
    st = {}
    d_ff = w1_ref.shape[1]

    def head():
        st["hn"] = _rms(load_x(), gpre_ref[...]).astype(BF16)
        st["acc"] = None

    def chunk(c):
        def run():
            cols = slice(c * MLP_CHUNK, (c + 1) * MLP_CHUNK)
            h = jnp.dot(st["hn"], w1_ref[:, cols], preferred_element_type=F32)
            a = jnp.maximum(h, 0.0)
            a = (a * a).astype(BF16)
            p = jnp.dot(a, w2_ref[cols, :], preferred_element_type=F32)
            st["acc"] = p if st["acc"] is None else st["acc"] + p
        return run

    def tail():
        out_ref[...] = load_x() + _rms(st["acc"], gpost_ref[...])

    return [head] + [chunk(c) for c in range(d_ff // MLP_CHUNK)] + [tail]


def _skew_slots(step, x1_ref):
    new = step % 2

    @pl.when(step == 0)
    def _():
        x1_ref[1] = jnp.zeros(x1_ref.shape[1:], F32)

    return 1 - new, new


def _pool_pieces(x_ref, gpre_ref, gpost_ref, w_ref, b_ref, scale_ref, halo_ref, store, first_row):
    st = {"parts": []}
    ts, d = x_ref.shape
    cg = d // len(POOL_WINDOWS)

    def head():
        x = x_ref[...]
        hn = _rms(x, gpre_ref[...])
        st["hn"] = hn
        st["ext"] = jnp.concatenate([halo_ref[...], hn], axis=0)
        halo_ref[...] = hn[ts - POOL_HALO:, :]
        st["t"] = (first_row + lax.broadcasted_iota(jnp.int32, (ts, 1), 0)).astype(F32)

    def group(g, w):
        def run():
            e = st["ext"][:, g * cg:(g + 1) * cg]
            sh = 1
            while sh < w:
                e = e + pltpu.roll(e, sh, axis=0)
                sh *= 2
            inv_cnt = 1.0 / jnp.minimum(st["t"] + 1.0, float(w))
            pooled = e[POOL_HALO:, :] * inv_cnt - st["hn"][:, g * cg:(g + 1) * cg]
            st["parts"].append(jnp.dot(pooled.astype(BF16), w_ref[g], preferred_element_type=F32))
        return run

    def tail():
        y = jnp.concatenate(st["parts"], axis=-1)
        y = (y + b_ref[...]) * scale_ref[...]
        store(x_ref[...] + _rms(y, gpost_ref[...]))

    return [head] + [group(g, w) for g, w in enumerate(POOL_WINDOWS)] + [tail]


def _pool_mlp_kernel(x_ref, gpre_ref, gpost_ref, w_ref, b_ref, scale_ref,
                     fpre_ref, fpost_ref, w1_ref, w2_ref, out_ref, halo_ref, x1_ref,
                     *, tiles_per_seq):
    step = pl.program_id(0)
    ts = x_ref.shape[0]
    old, new = _skew_slots(step, x1_ref)

    @pl.when(step % tiles_per_seq == 0)
    def _():
        halo_ref[...] = jnp.zeros_like(halo_ref)

    def store(v):
        x1_ref[new] = v

    mlp = _mlp_pieces(lambda: x1_ref[old], fpre_ref, fpost_ref, w1_ref, w2_ref, out_ref)
    pool = _pool_pieces(x_ref, gpre_ref, gpost_ref, w_ref, b_ref, scale_ref, halo_ref, store,
                        (step % tiles_per_seq) * ts)
    _interleave(mlp, pool)


def _pool_mlp_layer(x2, gpre, gpost, w, b, scale, fpre, fpost, w1, w2, *, seq_len, ts=ROW_TILE):
    t, d = x2.shape
    n_tiles = t // ts
    return pl.pallas_call(
        functools.partial(_pool_mlp_kernel, tiles_per_seq=seq_len // ts),
        grid=(n_tiles + 1,),
        in_specs=[
            pl.BlockSpec((ts, d), lambda i: (jnp.minimum(i, n_tiles - 1), 0)),
            _resident((1, d)), _resident((1, d)), _resident(w.shape),
            _resident((1, d)), _resident((1, d)),
            _resident((1, d)), _resident((1, d)), _resident(w1.shape), _resident(w2.shape),
        ],
        out_specs=pl.BlockSpec((ts, d), lambda i: (jnp.maximum(i - 1, 0), 0)),
        out_shape=jax.ShapeDtypeStruct((t, d), F32),
        scratch_shapes=[
            pltpu.VMEM((POOL_HALO, d), F32),
            pltpu.VMEM((2, ts, d), F32),
        ],
        compiler_params=pltpu.CompilerParams(
            dimension_semantics=("arbitrary",), vmem_limit_bytes=VMEM_LIMIT_BYTES),
        name="pool_mlp_layer",
    )(x2, gpre, gpost, w, b, scale, fpre, fpost, w1, w2)


def _mlstm_proj_kernel(x_ref, gpre_ref, w_ref, wg_ref, gb_ref, qk_ref, v_ref, o_ref, g_ref):
    d = x_ref.shape[1]
    hn = _rms(x_ref[...], gpre_ref[...]).astype(BF16)
    for blk in range(4 * d // PROJ_BLOCK):
        part, off = divmod(blk * PROJ_BLOCK, d)
        cols = slice(blk * PROJ_BLOCK, (blk + 1) * PROJ_BLOCK)
        oc = slice(off, off + PROJ_BLOCK)
        pre = jnp.dot(hn, w_ref[:, cols], preferred_element_type=F32)
        if part < 2:
            qk_ref[:, cols] = pre
        elif part == 2:
            v_ref[:, oc] = pre.astype(BF16)
        else:
            o_ref[:, oc] = pre
    g_ref[...] = jnp.dot(hn, wg_ref[...], preferred_element_type=F32) + gb_ref[...]


def _mlstm_proj_layer(x2, gpre, w_main, w_gate, gate_bias, *, ts=ROW_TILE):
    t, d = x2.shape
    tile = lambda width: pl.BlockSpec((ts, width), lambda i: (i, 0))
    return pl.pallas_call(
        _mlstm_proj_kernel,
        grid=(t // ts,),
        in_specs=[tile(d), _resident((1, d)), _resident(w_main.shape), _resident(w_gate.shape),
                  _resident(gate_bias.shape)],
        out_specs=[tile(2 * d), tile(d), tile(d), tile(2 * LANES)],
        out_shape=[
            jax.ShapeDtypeStruct((t, 2 * d), F32),
            jax.ShapeDtypeStruct((t, d), BF16),
            jax.ShapeDtypeStruct((t, d), F32),
            jax.ShapeDtypeStruct((t, 2 * LANES), F32),
        ],
        compiler_params=pltpu.CompilerParams(
            dimension_semantics=("arbitrary",), vmem_limit_bytes=VMEM_LIMIT_BYTES),
        name="mlstm_proj_layer",
    )(x2, gpre, w_main, w_gate, gate_bias)


def _scan_rows(x, op, fill):
    n = x.shape[0]
    row = lax.broadcasted_iota(jnp.int32, x.shape, 0)
    sh = 1
    while sh < n:
        x = op(x, jnp.where(row >= sh, pltpu.roll(x, sh, axis=0), fill))
        sh *= 2
    return x


def _log_sigmoid(x):
    return -(jnp.maximum(-x, 0.0) + jnp.log1p(jnp.exp(-jnp.abs(x))))


def _chunk_gates(g_ref, m_ref, rows):
    chunk = rows.stop - rows.start
    ig = g_ref[rows, :LANES]
    b = _scan_rows(_log_sigmoid(g_ref[rows, LANES:]), jnp.add, 0.0)
    r = ig - b
    pm = _scan_rows(r, jnp.maximum, -jnp.inf)
    m_prev = m_ref[...]
    u = jnp.maximum(pm, m_prev)
    g = b[chunk - 1:chunk, :]
    a = g + r
    m_new = jnp.maximum(g + m_prev, jnp.max(a, axis=0, keepdims=True))
    m_ref[...] = m_new
    return dict(
        u=u,
        inter=jnp.exp(m_prev - u),
        inv_floor=jnp.exp(-(b + u)),
        decay=jnp.exp(g + m_prev - m_new),
        wts=jnp.exp(a - m_new),
        r_t=r.T,
    )


def _head_scores(h, rows, gt, causal, q_ref, k_ref):
    dh = q_ref.shape[-1] // N_HEADS
    cs = slice(h * dh, (h + 1) * dh)
    hl = slice(h, h + 1)
    s = lax.dot_general(q_ref[rows, cs], k_ref[rows, cs].astype(BF16), (((1,), (1,)), ((), ())),
                        preferred_element_type=F32)
    arg = jnp.where(causal, gt["r_t"][hl, :] - gt["u"][:, hl], -jnp.inf)
    return s * jnp.exp(arg)


def _head_finish(h, rows, gt, s, q_ref, k_ref, v_ref, c_ref, n_ref, h_ref):
    dh = q_ref.shape[-1] // N_HEADS
    cs = slice(h * dh, (h + 1) * dh)
    hl = slice(h, h + 1)
    qh = q_ref[rows, cs]
    kf = k_ref[rows, cs]
    vh = v_ref[rows, cs]
    c_old = c_ref[h]
    n_old = n_ref[hl, :]
    inter_h = gt["inter"][:, hl]
    num = (jnp.dot(s.astype(BF16), vh, preferred_element_type=F32)
           + inter_h * jnp.dot(qh, c_old.astype(BF16), preferred_element_type=F32))
    qn = jnp.sum(qh.astype(F32) * n_old.astype(BF16).astype(F32), axis=-1, keepdims=True)
    den = jnp.sum(s, axis=-1, keepdims=True) + inter_h * qn
    rinv = 1.0 / jnp.maximum(jnp.abs(den), gt["inv_floor"][:, hl])
    ms = jnp.mean(num * num, axis=-1, keepdims=True)
    h_ref[rows, cs] = num * (rinv * lax.rsqrt(rinv * rinv * ms + RMS_EPS))
    kw = kf * gt["wts"][:, hl]
    decay_h = gt["decay"][:, hl]
    c_ref[h] = decay_h * c_old + lax.dot_general(
        kw.astype(BF16), vh, (((0,), (0,)), ((), ())), preferred_element_type=F32)
    n_ref[hl, :] = decay_h * n_old + jnp.sum(kw, axis=0, keepdims=True)


def _conv_block(blk, qk_ref, cw_ref, cb_ref, hist_ref, q_ref, k_ref):
    ts, d = q_ref.shape
    dh = d // N_HEADS
    part, off = divmod(blk * PROJ_BLOCK, d)
    cols = slice(blk * PROJ_BLOCK, (blk + 1) * PROJ_BLOCK)
    oc = slice(off, off + PROJ_BLOCK)
    pre = qk_ref[:, cols]
    ext = jnp.concatenate([hist_ref[:, cols], pre], axis=0)
    hist_ref[:, cols] = pre[ts - CONV_HALO:, :]
    conv = cb_ref[:, cols]
    for tap in range(CONV_WIDTH):
        back = CONV_WIDTH - 1 - tap
        shifted = ext if back == 0 else pltpu.roll(ext, back, axis=0)
        conv = conv + cw_ref[tap:tap + 1, cols] * shifted[CONV_HALO:, :]
    act = conv * jax.nn.sigmoid(conv)
    if part == 0:
        q_ref[:, oc] = act.astype(BF16)
    else:
        k_ref[:, oc] = act * (dh ** -0.5)


def _recur_pieces(x_ref, qk_ref, v_ref, o_ref, g_ref, cw_ref, cb_ref, gpost_ref, hnw_ref, wout_ref,
                  hist_ref, q_ref, k_ref, c_ref, n_ref, m_ref, h_ref, store):
    ts, d = x_ref.shape
    chunk = MLSTM_CHUNK
    causal = (lax.broadcasted_iota(jnp.int32, (chunk, chunk), 0)
              >= lax.broadcasted_iota(jnp.int32, (chunk, chunk), 1))
    units = [(c, h) for c in range(ts // chunk) for h in range(N_HEADS)]
    rows_of = lambda c: slice(c * chunk, (c + 1) * chunk)
    gates, scores = {}, {}

    def prepare(u):
        if u >= len(units):
            return
        c, h = units[u]
        if c not in gates:
            gates[c] = _chunk_gates(g_ref, m_ref, rows_of(c))
        scores[u] = _head_scores(h, rows_of(c), gates[c], causal, q_ref, k_ref)

    def unit(u):
        def run():
            prepare(u + 1)
            c, h = units[u]
            _head_finish(h, rows_of(c), gates[c], scores.pop(u), q_ref, k_ref, v_ref,
                         c_ref, n_ref, h_ref)
        return run

    def tail():
        gated = jax.nn.sigmoid(o_ref[...]) * (h_ref[...] * hnw_ref[...])
        y = jnp.dot(gated.astype(BF16), wout_ref[...], preferred_element_type=F32)
        store(x_ref[...] + _rms(y, gpost_ref[...]))

    conv = [functools.partial(_conv_block, blk, qk_ref, cw_ref, cb_ref, hist_ref, q_ref, k_ref)
            for blk in range(2 * d // PROJ_BLOCK)]
    return conv + [lambda: prepare(0)] + [unit(u) for u in range(len(units))] + [tail]


def _mlstm_mlp_kernel(x_ref, qk_ref, v_ref, o_ref, g_ref, cw_ref, cb_ref, gpost_ref, hnw_ref,
                      wout_ref, fpre_ref, fpost_ref, w1_ref, w2_ref, out_ref,
                      hist_ref, q_ref, k_ref, c_ref, n_ref, m_ref, h_ref, x1_ref, *, tiles_per_seq):
    step = pl.program_id(0)
    old, new = _skew_slots(step, x1_ref)

    @pl.when(step % tiles_per_seq == 0)
    def _():
        hist_ref[...] = jnp.zeros_like(hist_ref)
        c_ref[...] = jnp.zeros_like(c_ref)
        n_ref[...] = jnp.zeros_like(n_ref)
        m_ref[...] = jnp.zeros_like(m_ref)

    def store(v):
        x1_ref[new] = v

    mlp = _mlp_pieces(lambda: x1_ref[old], fpre_ref, fpost_ref, w1_ref, w2_ref, out_ref)
    rec = _recur_pieces(x_ref, qk_ref, v_ref, o_ref, g_ref, cw_ref, cb_ref, gpost_ref, hnw_ref,
                        wout_ref, hist_ref, q_ref, k_ref, c_ref, n_ref, m_ref, h_ref, store)
    _interleave(mlp, rec)


def _mlstm_mlp_layer(x2, qk, v, o, g, conv_w, conv_b, gpost, hnw, w_out, fpre, fpost, w1, w2,
                     *, seq_len, ts=ROW_TILE):
    t, d = x2.shape
    dh = d // N_HEADS
    n_tiles = t // ts
    tile = lambda width: pl.BlockSpec((ts, width), lambda i: (jnp.minimum(i, n_tiles - 1), 0))
    return pl.pallas_call(
        functools.partial(_mlstm_mlp_kernel, tiles_per_seq=seq_len // ts),
        grid=(n_tiles + 1,),
        in_specs=[
            tile(d), tile(2 * d), tile(d), tile(d), tile(2 * LANES),
            _resident(conv_w.shape), _resident(conv_b.shape),
            _resident((1, d)), _resident((1, d)), _resident(w_out.shape),
            _resident((1, d)), _resident((1, d)), _resident(w1.shape), _resident(w2.shape),
        ],
        out_specs=pl.BlockSpec((ts, d), lambda i: (jnp.maximum(i - 1, 0), 0)),
        out_shape=jax.ShapeDtypeStruct((t, d), F32),
        scratch_shapes=[
            pltpu.VMEM((CONV_HALO, 2 * d), F32),
            pltpu.VMEM((ts, d), BF16),
            pltpu.VMEM((ts, d), F32),
            pltpu.VMEM((N_HEADS, dh, dh), F32),
            pltpu.VMEM((N_HEADS, dh), F32),
            pltpu.VMEM((1, LANES), F32),
            pltpu.VMEM((ts, d), F32),
            pltpu.VMEM((2, ts, d), F32),
        ],
        compiler_params=pltpu.CompilerParams(
            dimension_semantics=("arbitrary",), vmem_limit_bytes=VMEM_LIMIT_BYTES),
        name="mlstm_mlp_layer",
    )(x2, qk, v, o, g, conv_w, conv_b, gpost, hnw, w_out, fpre, fpost, w1, w2)


def _pad_gates(w_in, i_bias, f_bias, d):
    h = N_HEADS
    wg = jnp.zeros((d, 2 * LANES), F32)
    wg = wg.at[:, :h].set(w_in[:, 4 * d:4 * d + h])
    wg = wg.at[:, LANES:LANES + h].set(w_in[:, 4 * d + h:])
    gb = jnp.zeros((1, 2 * LANES), F32)
    gb = gb.at[0, :h].set(i_bias)
    gb = gb.at[0, LANES:LANES + h].set(f_bias)
    return wg.astype(BF16), gb


def kernel(x, mix_pre_g, mix_post_g, ffn_pre_g, ffn_post_g, pool_w, pool_b, pool_scale,
           mlstm_w_in, mlstm_conv_w, mlstm_conv_b, mlstm_i_bias, mlstm_f_bias,
           mlstm_head_norm_w, mlstm_w_out, mlp_w1, mlp_w2):
    bsz, s, d = x.shape
    depth = mix_pre_g.shape[0]
    row = lambda a: a.reshape(1, -1)
    pool_w16 = pool_w.astype(BF16)
    w1_16 = mlp_w1.astype(BF16)
    w2_16 = mlp_w2.astype(BF16)
    w_out16 = mlstm_w_out.astype(BF16)
    x2 = x.reshape(bsz * s, d)
    for i in range(depth):
        j = i // 2
        mlp_args = (row(ffn_pre_g[i]), row(ffn_post_g[i]), w1_16[i], w2_16[i])
        if i % 2 == 0:
            x2 = _pool_mlp_layer(x2, row(mix_pre_g[i]), row(mix_post_g[i]), pool_w16[j],
                                 row(pool_b[j]), row(pool_scale[j]), *mlp_args, seq_len=s)
        else:
            w_in = mlstm_w_in[j]
            w_gate, gate_bias = _pad_gates(w_in, mlstm_i_bias[j], mlstm_f_bias[j], d)
            qk, v, o, g = _mlstm_proj_layer(
                x2, row(mix_pre_g[i]), w_in[:, :4 * d].astype(BF16), w_gate, gate_bias)
            x2 = _mlstm_mlp_layer(x2, qk, v, o, g, mlstm_conv_w[j], row(mlstm_conv_b[j]),
                                  row(mix_post_g[i]), row(mlstm_head_norm_w[j]), w_out16[j],
                                  *mlp_args, seq_len=s)
    return x2.reshape(bsz, s, d)
```

```python
import functools

import jax
import jax.numpy as jnp
from jax import lax
from jax.experimental import pallas as pl
from jax.experimental.pallas import tpu as pltpu

F32 = jnp.float32
BF16 = jnp.bfloat16

RMS_EPS = 1e-6
POOL_WINDOWS = (2, 4, 8, 16)
N_HEADS = 8
CONV_WIDTH = 4

LANES = 128
SUBLANES = 8
VMEM_LIMIT_BYTES = 60 * 1024 * 1024

POOL_HALO = 16
CONV_HALO = SUBLANES
MLSTM_CHUNK = 128
ROW_TILE = 512
MLP_CHUNK = 512
PROJ_BLOCK = 256
CAST_CHUNKS = 64


def _rms(x, g):
    ms = jnp.mean(x * x, axis=-1, keepdims=True)
    return x * lax.rsqrt(ms + RMS_EPS) * g


def _resident(shape):
    nd = len(shape)
    return pl.BlockSpec(shape, lambda *_: (0,) * nd, pipeline_mode=pl.Buffered(1))


def _interleave(*streams):
    n = max(len(s) for s in streams)
    pos = [0] * len(streams)
    for i in range(n):
        for k, s in enumerate(streams):
            upto = -(-(i + 1) * len(s) // n)
            while pos[k] < upto:
                s[pos[k]]()
                pos[k] += 1


def _with_casts(body, n_in, n_out, n_cast):
    def kernel(*refs):
        a, b, c = n_in + n_cast, n_in + n_cast + n_out, n_in + 2 * n_cast + n_out
        for src, dst in zip(refs[n_in:a], refs[b:c]):
            dst[...] = src[:, :dst.shape[1]].astype(BF16)
        body(*refs[:n_in], *refs[a:b], *refs[c:])
    return kernel


def _cast_specs(casts, n_steps):
    in_specs, out_specs, out_shapes, args = [], [], [], []
    for stacked, layer, cols_out in casts:
        _, rows, cols = stacked.shape
        chunk = rows // CAST_CHUNKS
        pick = lambda i: jnp.minimum(i, CAST_CHUNKS - 1)
        assert n_steps >= CAST_CHUNKS and chunk * CAST_CHUNKS == rows and chunk % 16 == 0
        in_specs.append(pl.BlockSpec((None, chunk, cols), lambda i, l=layer: (l, pick(i), 0)))
        out_specs.append(pl.BlockSpec((chunk, cols_out), lambda i: (pick(i), 0)))
        out_shapes.append(jax.ShapeDtypeStruct((rows, cols_out), BF16))
        args.append(stacked)
    return in_specs, out_specs, out_shapes, args


def _mlp_pieces(load_x, gpre_ref, gpost_ref, w1_ref, w2_ref, out_ref):
    st = {}
    d_ff = w1_ref.shape[1]

    def head():
        st["hn"] = _rms(load_x(), gpre_ref[...]).astype(BF16)
        st["acc"] = None

    def chunk(c):
        def run():
            cols = slice(c * MLP_CHUNK, (c + 1) * MLP_CHUNK)
            h = jnp.dot(st["hn"], w1_ref[:, cols], preferred_element_type=F32)
            a = jnp.maximum(h, 0.0)
            a = (a * a).astype(BF16)
            p = jnp.dot(a, w2_ref[cols, :], preferred_element_type=F32)
            st["acc"] = p if st["acc"] is None else st["acc"] + p
        return run

    def tail():
        out_ref[...] = load_x() + _rms(st["acc"], gpost_ref[...])

    return [head] + [chunk(c) for c in range(d_ff // MLP_CHUNK)] + [tail]


def _skew_slots(step, x1_ref):
    new = step % 2

    @pl.when(step == 0)
    def _():
        x1_ref[1] = jnp.zeros(x1_ref.shape[1:], F32)

    return 1 - new, new


def _pool_pieces(x_ref, gpre_ref, gpost_ref, w_ref, b_ref, scale_ref, halo_ref, store, first_row):
    st = {"parts": []}
    ts, d = x_ref.shape
    cg = d // len(POOL_WINDOWS)

    def head():
        x = x_ref[...]
        hn = _rms(x, gpre_ref[...])
        st["hn"] = hn
        st["ext"] = jnp.concatenate([halo_ref[...], hn], axis=0)
        halo_ref[...] = hn[ts - POOL_HALO:, :]
        st["t"] = (first_row + lax.broadcasted_iota(jnp.int32, (ts, 1), 0)).astype(F32)

    def group(g, w):
        def run():
            e = st["ext"][:, g * cg:(g + 1) * cg]
            sh = 1
            while sh < w:
                e = e + pltpu.roll(e, sh, axis=0)
                sh *= 2
            inv_cnt = 1.0 / jnp.minimum(st["t"] + 1.0, float(w))
            pooled = e[POOL_HALO:, :] * inv_cnt - st["hn"][:, g * cg:(g + 1) * cg]
            st["parts"].append(jnp.dot(pooled.astype(BF16), w_ref[g], preferred_element_type=F32))
        return run

    def tail():
        y = jnp.concatenate(st["parts"], axis=-1)
        y = (y + b_ref[...]) * scale_ref[...]
        store(x_ref[...] + _rms(y, gpost_ref[...]))

    return [head] + [group(g, w) for g, w in enumerate(POOL_WINDOWS)] + [tail]


def _pool_mlp_kernel(x_ref, gpre_ref, gpost_ref, w_ref, b_ref, scale_ref,
                     fpre_ref, fpost_ref, w1_ref, w2_ref, out_ref, halo_ref, x1_ref,
                     *, tiles_per_seq):
    step = pl.program_id(0)
    ts = x_ref.shape[0]
    old, new = _skew_slots(step, x1_ref)

    @pl.when(step % tiles_per_seq == 0)
    def _():
        halo_ref[...] = jnp.zeros_like(halo_ref)

    def store(v):
        x1_ref[new] = v

    mlp = _mlp_pieces(lambda: x1_ref[old], fpre_ref, fpost_ref, w1_ref, w2_ref, out_ref)
    pool = _pool_pieces(x_ref, gpre_ref, gpost_ref, w_ref, b_ref, scale_ref, halo_ref, store,
                        (step % tiles_per_seq) * ts)
    _interleave(mlp, pool)


def _pool_mlp_layer(x2, gpre, gpost, w, b, scale, fpre, fpost, w1, w2, *, seq_len, casts=(),
                    ts=ROW_TILE):
    t, d = x2.shape
    n_tiles = t // ts
    c_in, c_out, c_shapes, c_args = _cast_specs(casts, n_tiles + 1)
    outs = pl.pallas_call(
        _with_casts(functools.partial(_pool_mlp_kernel, tiles_per_seq=seq_len // ts),
                    10, 1, len(casts)),
        grid=(n_tiles + 1,),
        in_specs=[
            pl.BlockSpec((ts, d), lambda i: (jnp.minimum(i, n_tiles - 1), 0)),
            _resident((1, d)), _resident((1, d)), _resident(w.shape),
            _resident((1, d)), _resident((1, d)),
            _resident((1, d)), _resident((1, d)), _resident(w1.shape), _resident(w2.shape),
        ] + c_in,
        out_specs=[pl.BlockSpec((ts, d), lambda i: (jnp.maximum(i - 1, 0), 0))] + c_out,
        out_shape=[jax.ShapeDtypeStruct((t, d), F32)] + c_shapes,
        scratch_shapes=[
            pltpu.VMEM((POOL_HALO, d), F32),
            pltpu.VMEM((2, ts, d), F32),
        ],
        compiler_params=pltpu.CompilerParams(
            dimension_semantics=("arbitrary",), vmem_limit_bytes=VMEM_LIMIT_BYTES),
        name="pool_mlp_layer",
    )(x2, gpre, gpost, w, b, scale, fpre, fpost, w1, w2, *c_args)
    return outs[0], outs[1:]


def _mlstm_proj_kernel(x_ref, gpre_ref, w_ref, wg_ref, gb_ref, qk_ref, v_ref, o_ref, g_ref):
    d = x_ref.shape[1]
    hn = _rms(x_ref[...], gpre_ref[...]).astype(BF16)
    for blk in range(4 * d // PROJ_BLOCK):
        part, off = divmod(blk * PROJ_BLOCK, d)
        cols = slice(blk * PROJ_BLOCK, (blk + 1) * PROJ_BLOCK)
        oc = slice(off, off + PROJ_BLOCK)
        pre = jnp.dot(hn, w_ref[:, cols], preferred_element_type=F32)
        if part < 2:
            qk_ref[:, cols] = pre
        elif part == 2:
            v_ref[:, oc] = pre.astype(BF16)
        else:
            o_ref[:, oc] = pre
    g_ref[...] = jnp.dot(hn, wg_ref[...], preferred_element_type=F32) + gb_ref[...]


def _mlstm_proj_layer(x2, gpre, w_main, w_gate, gate_bias, *, casts=(), ts=ROW_TILE):
    t, d = x2.shape
    tile = lambda width: pl.BlockSpec((ts, width), lambda i: (i, 0))
    c_in, c_out, c_shapes, c_args = _cast_specs(casts, t // ts)
    outs = pl.pallas_call(
        _with_casts(_mlstm_proj_kernel, 5, 4, len(casts)),
        grid=(t // ts,),
        in_specs=[tile(d), _resident((1, d)), _resident(w_main.shape), _resident(w_gate.shape),
                  _resident(gate_bias.shape)] + c_in,
        out_specs=[tile(2 * d), tile(d), tile(d), tile(2 * LANES)] + c_out,
        out_shape=[
            jax.ShapeDtypeStruct((t, 2 * d), F32),
            jax.ShapeDtypeStruct((t, d), BF16),
            jax.ShapeDtypeStruct((t, d), F32),
            jax.ShapeDtypeStruct((t, 2 * LANES), F32),
        ] + c_shapes,
        compiler_params=pltpu.CompilerParams(
            dimension_semantics=("arbitrary",), vmem_limit_bytes=VMEM_LIMIT_BYTES),
        name="mlstm_proj_layer",
    )(x2, gpre, w_main, w_gate, gate_bias, *c_args)
    return outs[:4], outs[4:]


def _scan_rows(x, op, fill):
    n = x.shape[0]
    row = lax.broadcasted_iota(jnp.int32, x.shape, 0)
    sh = 1
    while sh < n:
        x = op(x, jnp.where(row >= sh, pltpu.roll(x, sh, axis=0), fill))
        sh *= 2
    return x


def _log_sigmoid(x):
    return -(jnp.maximum(-x, 0.0) + jnp.log1p(jnp.exp(-jnp.abs(x))))


def _chunk_gates(g_ref, m_ref, rows):
    chunk = rows.stop - rows.start
    ig = g_ref[rows, :LANES]
    b = _scan_rows(_log_sigmoid(g_ref[rows, LANES:]), jnp.add, 0.0)
    r = ig - b
    pm = _scan_rows(r, jnp.maximum, -jnp.inf)
    m_prev = m_ref[...]
    u = jnp.maximum(pm, m_prev)
    g = b[chunk - 1:chunk, :]
    a = g + r
    m_new = jnp.maximum(g + m_prev, jnp.max(a, axis=0, keepdims=True))
    m_ref[...] = m_new
    return dict(
        u=u,
        inter=jnp.exp(m_prev - u),
        inv_floor=jnp.exp(-(b + u)),
        decay=jnp.exp(g + m_prev - m_new),
        wts=jnp.exp(a - m_new),
        r_t=r.T,
    )


def _head_scores(h, rows, gt, causal, q_ref, k_ref):
    dh = q_ref.shape[-1] // N_HEADS
    cs = slice(h * dh, (h + 1) * dh)
    hl = slice(h, h + 1)
    s = lax.dot_general(q_ref[rows, cs], k_ref[rows, cs].astype(BF16), (((1,), (1,)), ((), ())),
                        preferred_element_type=F32)
    arg = jnp.where(causal, gt["r_t"][hl, :] - gt["u"][:, hl], -jnp.inf)
    return s * jnp.exp(arg)


def _head_finish(h, rows, gt, s, q_ref, k_ref, v_ref, c_ref, n_ref, h_ref):
    dh = q_ref.shape[-1] // N_HEADS
    cs = slice(h * dh, (h + 1) * dh)
    hl = slice(h, h + 1)
    qh = q_ref[rows, cs]
    kf = k_ref[rows, cs]
    vh = v_ref[rows, cs]
    c_old = c_ref[h]
    n_old = n_ref[hl, :]
    inter_h = gt["inter"][:, hl]
    num = (jnp.dot(s.astype(BF16), vh, preferred_element_type=F32)
           + inter_h * jnp.dot(qh, c_old.astype(BF16), preferred_element_type=F32))
    qn = jnp.sum(qh.astype(F32) * n_old.astype(BF16).astype(F32), axis=-1, keepdims=True)
    den = jnp.sum(s, axis=-1, keepdims=True) + inter_h * qn
    rinv = 1.0 / jnp.maximum(jnp.abs(den), gt["inv_floor"][:, hl])
    ms = jnp.mean(num * num, axis=-1, keepdims=True)
    h_ref[rows, cs] = num * (rinv * lax.rsqrt(rinv * rinv * ms + RMS_EPS))
    kw = kf * gt["wts"][:, hl]
    decay_h = gt["decay"][:, hl]
    c_ref[h] = decay_h * c_old + lax.dot_general(
        kw.astype(BF16), vh, (((0,), (0,)), ((), ())), preferred_element_type=F32)
    n_ref[hl, :] = decay_h * n_old + jnp.sum(kw, axis=0, keepdims=True)


def _conv_block(blk, qk_ref, cw_ref, cb_ref, hist_ref, q_ref, k_ref):
    ts, d = q_ref.shape
    dh = d // N_HEADS
    part, off = divmod(blk * PROJ_BLOCK, d)
    cols = slice(blk * PROJ_BLOCK, (blk + 1) * PROJ_BLOCK)
    oc = slice(off, off + PROJ_BLOCK)
    pre = qk_ref[:, cols]
    ext = jnp.concatenate([hist_ref[:, cols], pre], axis=0)
    hist_ref[:, cols] = pre[ts - CONV_HALO:, :]
    conv = cb_ref[:, cols]
    for tap in range(CONV_WIDTH):
        back = CONV_WIDTH - 1 - tap
        shifted = ext if back == 0 else pltpu.roll(ext, back, axis=0)
        conv = conv + cw_ref[tap:tap + 1, cols] * shifted[CONV_HALO:, :]
    act = conv * jax.nn.sigmoid(conv)
    if part == 0:
        q_ref[:, oc] = act.astype(BF16)
    else:
        k_ref[:, oc] = act * (dh ** -0.5)


def _recur_pieces(x_ref, qk_ref, v_ref, o_ref, g_ref, cw_ref, cb_ref, gpost_ref, hnw_ref, wout_ref,
                  hist_ref, q_ref, k_ref, c_ref, n_ref, m_ref, h_ref, store):
    ts, d = x_ref.shape
    chunk = MLSTM_CHUNK
    causal = (lax.broadcasted_iota(jnp.int32, (chunk, chunk), 0)
              >= lax.broadcasted_iota(jnp.int32, (chunk, chunk), 1))
    units = [(c, h) for c in range(ts // chunk) for h in range(N_HEADS)]
    rows_of = lambda c: slice(c * chunk, (c + 1) * chunk)
    gates, scores = {}, {}

    def prepare(u):
        if u >= len(units):
            return
        c, h = units[u]
        if c not in gates:
            gates[c] = _chunk_gates(g_ref, m_ref, rows_of(c))
        scores[u] = _head_scores(h, rows_of(c), gates[c], causal, q_ref, k_ref)

    def unit(u):
        def run():
            prepare(u + 1)
            c, h = units[u]
            _head_finish(h, rows_of(c), gates[c], scores.pop(u), q_ref, k_ref, v_ref,
                         c_ref, n_ref, h_ref)
        return run

    def tail():
        gated = jax.nn.sigmoid(o_ref[...]) * (h_ref[...] * hnw_ref[...])
        y = jnp.dot(gated.astype(BF16), wout_ref[...], preferred_element_type=F32)
        store(x_ref[...] + _rms(y, gpost_ref[...]))

    conv = [functools.partial(_conv_block, blk, qk_ref, cw_ref, cb_ref, hist_ref, q_ref, k_ref)
            for blk in range(2 * d // PROJ_BLOCK)]
    return conv + [lambda: prepare(0)] + [unit(u) for u in range(len(units))] + [tail]


def _mlstm_mlp_kernel(x_ref, qk_ref, v_ref, o_ref, g_ref, cw_ref, cb_ref, gpost_ref, hnw_ref,
                      wout_ref, fpre_ref, fpost_ref, w1_ref, w2_ref, out_ref,
                      hist_ref, q_ref, k_ref, c_ref, n_ref, m_ref, h_ref, x1_ref, *, tiles_per_seq):
    step = pl.program_id(0)
    old, new = _skew_slots(step, x1_ref)

    @pl.when(step % tiles_per_seq == 0)
    def _():
        hist_ref[...] = jnp.zeros_like(hist_ref)
        c_ref[...] = jnp.zeros_like(c_ref)
        n_ref[...] = jnp.zeros_like(n_ref)
        m_ref[...] = jnp.zeros_like(m_ref)

    def store(v):
        x1_ref[new] = v

    mlp = _mlp_pieces(lambda: x1_ref[old], fpre_ref, fpost_ref, w1_ref, w2_ref, out_ref)
    rec = _recur_pieces(x_ref, qk_ref, v_ref, o_ref, g_ref, cw_ref, cb_ref, gpost_ref, hnw_ref,
                        wout_ref, hist_ref, q_ref, k_ref, c_ref, n_ref, m_ref, h_ref, store)
    _interleave(mlp, rec)


def _mlstm_mlp_layer(x2, qk, v, o, g, conv_w, conv_b, gpost, hnw, w_out, fpre, fpost, w1, w2,
                     *, seq_len, ts=ROW_TILE):
    t, d = x2.shape
    dh = d // N_HEADS
    n_tiles = t // ts
    tile = lambda width: pl.BlockSpec((ts, width), lambda i: (jnp.minimum(i, n_tiles - 1), 0))
    return pl.pallas_call(
        functools.partial(_mlstm_mlp_kernel, tiles_per_seq=seq_len // ts),
        grid=(n_tiles + 1,),
        in_specs=[
            tile(d), tile(2 * d), tile(d), tile(d), tile(2 * LANES),
            _resident(conv_w.shape), _resident(conv_b.shape),
            _resident((1, d)), _resident((1, d)), _resident(w_out.shape),
            _resident((1, d)), _resident((1, d)), _resident(w1.shape), _resident(w2.shape),
        ],
        out_specs=pl.BlockSpec((ts, d), lambda i: (jnp.maximum(i - 1, 0), 0)),
        out_shape=jax.ShapeDtypeStruct((t, d), F32),
        scratch_shapes=[
            pltpu.VMEM((CONV_HALO, 2 * d), F32),
            pltpu.VMEM((ts, d), BF16),
            pltpu.VMEM((ts, d), F32),
            pltpu.VMEM((N_HEADS, dh, dh), F32),
            pltpu.VMEM((N_HEADS, dh), F32),
            pltpu.VMEM((1, LANES), F32),
            pltpu.VMEM((ts, d), F32),
            pltpu.VMEM((2, ts, d), F32),
        ],
        compiler_params=pltpu.CompilerParams(
            dimension_semantics=("arbitrary",), vmem_limit_bytes=VMEM_LIMIT_BYTES),
        name="mlstm_mlp_layer",
    )(x2, qk, v, o, g, conv_w, conv_b, gpost, hnw, w_out, fpre, fpost, w1, w2)


def _pad_gates(w_in, i_bias, f_bias, d):
    h = N_HEADS
    wg = jnp.zeros((d, 2 * LANES), F32)
    wg = wg.at[:, :h].set(w_in[:, 4 * d:4 * d + h])
    wg = wg.at[:, LANES:LANES + h].set(w_in[:, 4 * d + h:])
    gb = jnp.zeros((1, 2 * LANES), F32)
    gb = gb.at[0, :h].set(i_bias)
    gb = gb.at[0, LANES:LANES + h].set(f_bias)
    return wg.astype(BF16), gb


def kernel(x, mix_pre_g, mix_post_g, ffn_pre_g, ffn_post_g, pool_w, pool_b, pool_scale,
           mlstm_w_in, mlstm_conv_w, mlstm_conv_b, mlstm_i_bias, mlstm_f_bias,
           mlstm_head_norm_w, mlstm_w_out, mlp_w1, mlp_w2):
    bsz, s, d = x.shape
    depth = mix_pre_g.shape[0]
    d_ff = mlp_w1.shape[2]
    row = lambda a: a.reshape(1, -1)
    n_groups, cg = pool_w.shape[1], pool_w.shape[2]
    pool_w_rows = pool_w.reshape(pool_w.shape[0], n_groups * cg, cg)
    mlp16 = {0: (mlp_w1[0].astype(BF16), mlp_w2[0].astype(BF16))}
    pool16 = {0: pool_w_rows[0].astype(BF16)}
    mlstm16 = {}
    x2 = x.reshape(bsz * s, d)
    for i in range(depth):
        j = i // 2
        nxt = i + 1 < depth
        mlp_casts = [(mlp_w1, i + 1, d_ff), (mlp_w2, i + 1, d)] if nxt else []
        mlp_args = (row(ffn_pre_g[i]), row(ffn_post_g[i])) + mlp16[i]
        if i % 2 == 0:
            casts = ([(mlstm_w_in, j, 4 * d), (mlstm_w_out, j, d)] if nxt else []) + mlp_casts
            x2, cast = _pool_mlp_layer(
                x2, row(mix_pre_g[i]), row(mix_post_g[i]), pool16[j].reshape(n_groups, cg, cg),
                row(pool_b[j]), row(pool_scale[j]), *mlp_args, seq_len=s, casts=casts)
            if nxt:
                mlstm16[j], mlp16[i + 1] = (cast[0], cast[1]), (cast[2], cast[3])
        else:
            w_in16, w_out16 = mlstm16[j]
            w_gate, gate_bias = _pad_gates(mlstm_w_in[j], mlstm_i_bias[j], mlstm_f_bias[j], d)
            casts = ([(pool_w_rows, j + 1, cg)] + mlp_casts) if nxt else []
            (qk, v, o, g), cast = _mlstm_proj_layer(
                x2, row(mix_pre_g[i]), w_in16, w_gate, gate_bias, casts=casts)
            if nxt:
                pool16[j + 1], mlp16[i + 1] = cast[0], (cast[1], cast[2])
            x2 = _mlstm_mlp_layer(x2, qk, v, o, g, mlstm_conv_w[j], row(mlstm_conv_b[j]),
                                  row(mix_post_g[i]), row(mlstm_head_norm_w[j]), w_out16,
                                  *mlp_args, seq_len=s)
    return x2.reshape(bsz, s, d)
```

```python
import functools

import jax
import jax.numpy as jnp
from jax import lax
from jax.experimental import pallas as pl
from jax.experimental.pallas import tpu as pltpu

F32 = jnp.float32
BF16 = jnp.bfloat16

RMS_EPS = 1e-6
POOL_WINDOWS = (2, 4, 8, 16)
N_HEADS = 8
CONV_WIDTH = 4

LANES = 128
SUBLANES = 8
VMEM_LIMIT_BYTES = 60 * 1024 * 1024

POOL_HALO = 16
CONV_HALO = SUBLANES
MLSTM_CHUNK = 128
ROW_TILE = 512
MLP_CHUNK = 512
PROJ_BLOCK = 256
CAST_CHUNKS = 64


def _rms(x, g):
    ms = jnp.mean(x * x, axis=-1, keepdims=True)
    return x * lax.rsqrt(ms + RMS_EPS) * g


def _resident(shape):
    nd = len(shape)
    return pl.BlockSpec(shape, lambda *_: (0,) * nd, pipeline_mode=pl.Buffered(1))


def _interleave(*streams):
    n = max(len(s) for s in streams)
    pos = [0] * len(streams)
    for i in range(n):
        for k, s in enumerate(streams):
            upto = -(-(i + 1) * len(s) // n)
            while pos[k] < upto:
                s[pos[k]]()
                pos[k] += 1


def _with_casts(body, n_in, n_out, n_cast):
    def kernel(*refs):
        a, b, c = n_in + n_cast, n_in + n_cast + n_out, n_in + 2 * n_cast + n_out
        for src, dst in zip(refs[n_in:a], refs[b:c]):
            dst[...] = src[:, :dst.shape[1]].astype(BF16)
        body(*refs[:n_in], *refs[a:b], *refs[c:])
    return kernel


def _cast_specs(casts, n_steps):
    in_specs, out_specs, out_shapes, args = [], [], [], []
    for stacked, layer, cols_out in casts:
        _, rows, cols = stacked.shape
        chunk = rows // CAST_CHUNKS
        pick = lambda i: jnp.minimum(i, CAST_CHUNKS - 1)
        assert n_steps >= CAST_CHUNKS and chunk * CAST_CHUNKS == rows and chunk % 16 == 0
        in_specs.append(pl.BlockSpec((None, chunk, cols), lambda i, l=layer: (l, pick(i), 0)))
        out_specs.append(pl.BlockSpec((chunk, cols_out), lambda i: (pick(i), 0)))
        out_shapes.append(jax.ShapeDtypeStruct((rows, cols_out), BF16))
        args.append(stacked)
    return in_specs, out_specs, out_shapes, args


def _mlp_pieces(load_x, gpre_ref, gpost_ref, w1_ref, w2_ref, out_ref):
    st = {}
    d_ff = w1_ref.shape[1]

    def head():
        st["hn"] = _rms(load_x(), gpre_ref[...]).astype(BF16)
        st["acc"] = None

    def chunk(c):
        def run():
            cols = slice(c * MLP_CHUNK, (c + 1) * MLP_CHUNK)
            h = jnp.dot(st["hn"], w1_ref[:, cols], preferred_element_type=F32)
            a = jnp.maximum(h, 0.0)
            a = (a * a).astype(BF16)
            p = jnp.dot(a, w2_ref[cols, :], preferred_element_type=F32)
            st["acc"] = p if st["acc"] is None else st["acc"] + p
        return run

    def tail():
        out_ref[...] = load_x() + _rms(st["acc"], gpost_ref[...])

    return [head] + [chunk(c) for c in range(d_ff // MLP_CHUNK)] + [tail]


def _skew_slots(step, x1_ref):
    new = step % 2

    @pl.when(step == 0)
    def _():
        x1_ref[1] = jnp.zeros(x1_ref.shape[1:], F32)

    return 1 - new, new


def _pool_pieces(x_ref, gpre_ref, gpost_ref, w_ref, b_ref, scale_ref, halo_ref, store, first_row):
    st = {"parts": []}
    ts, d = x_ref.shape
    cg = d // len(POOL_WINDOWS)

    def head():
        x = x_ref[...]
        hn = _rms(x, gpre_ref[...])
        st["hn"] = hn
        st["ext"] = jnp.concatenate([halo_ref[...], hn], axis=0)
        halo_ref[...] = hn[ts - POOL_HALO:, :]
        st["t"] = (first_row + lax.broadcasted_iota(jnp.int32, (ts, 1), 0)).astype(F32)

    def group(g, w):
        def run():
            e = st["ext"][:, g * cg:(g + 1) * cg]
            sh = 1
            while sh < w:
                e = e + pltpu.roll(e, sh, axis=0)
                sh *= 2
            inv_cnt = 1.0 / jnp.minimum(st["t"] + 1.0, float(w))
            pooled = e[POOL_HALO:, :] * inv_cnt - st["hn"][:, g * cg:(g + 1) * cg]
            st["parts"].append(jnp.dot(pooled.astype(BF16), w_ref[g], preferred_element_type=F32))
        return run

    def tail():
        y = jnp.concatenate(st["parts"], axis=-1)
        y = (y + b_ref[...]) * scale_ref[...]
        store(x_ref[...] + _rms(y, gpost_ref[...]))

    return [head] + [group(g, w) for g, w in enumerate(POOL_WINDOWS)] + [tail]


def _pool_mlp_kernel(x_ref, gpre_ref, gpost_ref, w_ref, b_ref, scale_ref,
                     fpre_ref, fpost_ref, w1_ref, w2_ref, out_ref, halo_ref, x1_ref,
                     *, tiles_per_seq):
    step = pl.program_id(0)
    ts = x_ref.shape[0]
    old, new = _skew_slots(step, x1_ref)

    @pl.when(step % tiles_per_seq == 0)
    def _():
        halo_ref[...] = jnp.zeros_like(halo_ref)

    def store(v):
        x1_ref[new] = v

    mlp = _mlp_pieces(lambda: x1_ref[old], fpre_ref, fpost_ref, w1_ref, w2_ref, out_ref)
    pool = _pool_pieces(x_ref, gpre_ref, gpost_ref, w_ref, b_ref, scale_ref, halo_ref, store,
                        (step % tiles_per_seq) * ts)
    _interleave(mlp, pool)


def _pool_mlp_layer(x2, gpre, gpost, w, b, scale, fpre, fpost, w1, w2, *, seq_len, casts=(),
                    ts=ROW_TILE):
    t, d = x2.shape
    n_tiles = t // ts
    c_in, c_out, c_shapes, c_args = _cast_specs(casts, n_tiles + 1)
    outs = pl.pallas_call(
        _with_casts(functools.partial(_pool_mlp_kernel, tiles_per_seq=seq_len // ts),
                    10, 1, len(casts)),
        grid=(n_tiles + 1,),
        in_specs=[
            pl.BlockSpec((ts, d), lambda i: (jnp.minimum(i, n_tiles - 1), 0)),
            _resident((1, d)), _resident((1, d)), _resident(w.shape),
            _resident((1, d)), _resident((1, d)),
            _resident((1, d)), _resident((1, d)), _resident(w1.shape), _resident(w2.shape),
        ] + c_in,
        out_specs=[pl.BlockSpec((ts, d), lambda i: (jnp.maximum(i - 1, 0), 0))] + c_out,
        out_shape=[jax.ShapeDtypeStruct((t, d), F32)] + c_shapes,
        scratch_shapes=[
            pltpu.VMEM((POOL_HALO, d), F32),
            pltpu.VMEM((2, ts, d), F32),
        ],
        compiler_params=pltpu.CompilerParams(
            dimension_semantics=("arbitrary",), vmem_limit_bytes=VMEM_LIMIT_BYTES),
        name="pool_mlp_layer",
    )(x2, gpre, gpost, w, b, scale, fpre, fpost, w1, w2, *c_args)
    return outs[0], outs[1:]


def _mlstm_proj_kernel(x_ref, gpre_ref, w_ref, wg_ref, gb_ref, qk_ref, v_ref, o_ref, g_ref):
    d = x_ref.shape[1]
    hn = _rms(x_ref[...], gpre_ref[...]).astype(BF16)
    for blk in range(4 * d // PROJ_BLOCK):
        part, off = divmod(blk * PROJ_BLOCK, d)
        cols = slice(blk * PROJ_BLOCK, (blk + 1) * PROJ_BLOCK)
        oc = slice(off, off + PROJ_BLOCK)
        pre = jnp.dot(hn, w_ref[:, cols], preferred_element_type=F32)
        if part < 2:
            qk_ref[:, cols] = pre
        elif part == 2:
            v_ref[:, oc] = pre.astype(BF16)
        else:
            o_ref[:, oc] = pre
    g_ref[...] = jnp.dot(hn, wg_ref[...], preferred_element_type=F32) + gb_ref[...]


def _mlstm_proj_layer(x2, gpre, w_main, w_gate, gate_bias, *, casts=(), ts=ROW_TILE):
    t, d = x2.shape
    tile = lambda width: pl.BlockSpec((ts, width), lambda i: (i, 0))
    c_in, c_out, c_shapes, c_args = _cast_specs(casts, t // ts)
    outs = pl.pallas_call(
        _with_casts(_mlstm_proj_kernel, 5, 4, len(casts)),
        grid=(t // ts,),
        in_specs=[tile(d), _resident((1, d)), _resident(w_main.shape), _resident(w_gate.shape),
                  _resident(gate_bias.shape)] + c_in,
        out_specs=[tile(2 * d), tile(d), tile(d), tile(2 * LANES)] + c_out,
        out_shape=[
            jax.ShapeDtypeStruct((t, 2 * d), F32),
            jax.ShapeDtypeStruct((t, d), BF16),
            jax.ShapeDtypeStruct((t, d), F32),
            jax.ShapeDtypeStruct((t, 2 * LANES), F32),
        ] + c_shapes,
        compiler_params=pltpu.CompilerParams(
            dimension_semantics=("arbitrary",), vmem_limit_bytes=VMEM_LIMIT_BYTES),
        name="mlstm_proj_layer",
    )(x2, gpre, w_main, w_gate, gate_bias, *c_args)
    return outs[:4], outs[4:]


def _scan_rows(x, op, fill):
    n = x.shape[0]
    row = lax.broadcasted_iota(jnp.int32, x.shape, 0)
    sh = 1
    while sh < n:
        x = op(x, jnp.where(row >= sh, pltpu.roll(x, sh, axis=0), fill))
        sh *= 2
    return x


def _log_sigmoid(x):
    return -(jnp.maximum(-x, 0.0) + jnp.log1p(jnp.exp(-jnp.abs(x))))


def _chunk_gates(g_ref, m_ref, rows):
    chunk = rows.stop - rows.start
    ig = g_ref[rows, :LANES]
    b = _scan_rows(_log_sigmoid(g_ref[rows, LANES:]), jnp.add, 0.0)
    r = ig - b
    pm = _scan_rows(r, jnp.maximum, -jnp.inf)
    m_prev = m_ref[...]
    u = jnp.maximum(pm, m_prev)
    g = b[chunk - 1:chunk, :]
    a = g + r
    m_new = jnp.maximum(g + m_prev, jnp.max(a, axis=0, keepdims=True))
    m_ref[...] = m_new
    return dict(
        u=u,
        m_prev=m_prev,
        inv_floor=jnp.exp(-(b + u)),
        decay=jnp.exp(g + m_prev - m_new),
        wts=jnp.exp(a - m_new),
        r_t=r.T,
    )


def _head_scores(h, rows, gt, causal, q_ref, k_ref):
    dh = q_ref.shape[-1] // N_HEADS
    cs = slice(h * dh, (h + 1) * dh)
    hl = slice(h, h + 1)
    s = lax.dot_general(q_ref[rows, cs], k_ref[rows, cs].astype(BF16), (((1,), (1,)), ((), ())),
                        preferred_element_type=F32)
    u_bc = jnp.broadcast_to(gt["u"][:, hl], s.shape)
    arg = jnp.where(causal, gt["r_t"][hl, :] - u_bc, -jnp.inf)
    return s * jnp.exp(arg), jnp.exp(gt["m_prev"][:, hl] - u_bc)


def _head_finish(h, rows, gt, scored, q_ref, k_ref, v_ref, c_ref, n_ref, h_ref):
    s, inter = scored
    dh = q_ref.shape[-1] // N_HEADS
    cs = slice(h * dh, (h + 1) * dh)
    hl = slice(h, h + 1)
    qh = q_ref[rows, cs]
    kf = k_ref[rows, cs]
    vh = v_ref[rows, cs]
    c_old = c_ref[h]
    n_old = n_ref[hl, :]
    num = (jnp.dot(s.astype(BF16), vh, preferred_element_type=F32)
           + inter * jnp.dot(qh, c_old.astype(BF16), preferred_element_type=F32))
    qn = qh.astype(F32) * n_old.astype(BF16).astype(F32)
    den = jnp.sum(s + inter * qn, axis=-1, keepdims=True)
    rinv = 1.0 / jnp.maximum(jnp.abs(den), gt["inv_floor"][:, hl])
    ms = jnp.mean(num * num, axis=-1, keepdims=True)
    h_ref[rows, cs] = num * (rinv * lax.rsqrt(rinv * rinv * ms + RMS_EPS))
    kw = kf * gt["wts"][:, hl]
    decay_h = gt["decay"][:, hl]
    c_ref[h] = decay_h * c_old + lax.dot_general(
        kw.astype(BF16), vh, (((0,), (0,)), ((), ())), preferred_element_type=F32)
    n_ref[hl, :] = decay_h * n_old + jnp.sum(kw, axis=0, keepdims=True)


def _conv_block(blk, qk_ref, cw_ref, cb_ref, hist_ref, q_ref, k_ref):
    ts, d = q_ref.shape
    dh = d // N_HEADS
    part, off = divmod(blk * PROJ_BLOCK, d)
    cols = slice(blk * PROJ_BLOCK, (blk + 1) * PROJ_BLOCK)
    oc = slice(off, off + PROJ_BLOCK)
    pre = qk_ref[:, cols]
    ext = jnp.concatenate([hist_ref[:, cols], pre], axis=0)
    hist_ref[:, cols] = pre[ts - CONV_HALO:, :]
    conv = cb_ref[:, cols]
    for tap in range(CONV_WIDTH):
        back = CONV_WIDTH - 1 - tap
        shifted = ext if back == 0 else pltpu.roll(ext, back, axis=0)
        conv = conv + cw_ref[tap:tap + 1, cols] * shifted[CONV_HALO:, :]
    act = conv * jax.nn.sigmoid(conv)
    if part == 0:
        q_ref[:, oc] = act.astype(BF16)
    else:
        k_ref[:, oc] = act * (dh ** -0.5)


def _recur_pieces(x_ref, qk_ref, v_ref, o_ref, g_ref, cw_ref, cb_ref, gpost_ref, hnw_ref, wout_ref,
                  hist_ref, q_ref, k_ref, c_ref, n_ref, m_ref, h_ref, store):
    ts, d = x_ref.shape
    chunk = MLSTM_CHUNK
    causal = (lax.broadcasted_iota(jnp.int32, (chunk, chunk), 0)
              >= lax.broadcasted_iota(jnp.int32, (chunk, chunk), 1))
    units = [(c, h) for c in range(ts // chunk) for h in range(N_HEADS)]
    rows_of = lambda c: slice(c * chunk, (c + 1) * chunk)
    gates, scores = {}, {}

    def prepare(u):
        if u >= len(units):
            return
        c, h = units[u]
        if c not in gates:
            gates[c] = _chunk_gates(g_ref, m_ref, rows_of(c))
        scores[u] = _head_scores(h, rows_of(c), gates[c], causal, q_ref, k_ref)

    def unit(u):
        def run():
            prepare(u + 1)
            c, h = units[u]
            _head_finish(h, rows_of(c), gates[c], scores.pop(u), q_ref, k_ref, v_ref,
                         c_ref, n_ref, h_ref)
        return run

    def tail():
        gated = jax.nn.sigmoid(o_ref[...]) * (h_ref[...] * hnw_ref[...])
        y = jnp.dot(gated.astype(BF16), wout_ref[...], preferred_element_type=F32)
        store(x_ref[...] + _rms(y, gpost_ref[...]))

    conv = [functools.partial(_conv_block, blk, qk_ref, cw_ref, cb_ref, hist_ref, q_ref, k_ref)
            for blk in range(2 * d // PROJ_BLOCK)]
    return conv + [lambda: prepare(0)] + [unit(u) for u in range(len(units))] + [tail]


def _mlstm_mlp_kernel(x_ref, qk_ref, v_ref, o_ref, g_ref, cw_ref, cb_ref, gpost_ref, hnw_ref,
                      wout_ref, fpre_ref, fpost_ref, w1_ref, w2_ref, out_ref,
                      hist_ref, q_ref, k_ref, c_ref, n_ref, m_ref, h_ref, x1_ref, *, tiles_per_seq):
    step = pl.program_id(0)
    old, new = _skew_slots(step, x1_ref)

    @pl.when(step % tiles_per_seq == 0)
    def _():
        hist_ref[...] = jnp.zeros_like(hist_ref)
        c_ref[...] = jnp.zeros_like(c_ref)
        n_ref[...] = jnp.zeros_like(n_ref)
        m_ref[...] = jnp.zeros_like(m_ref)

    def store(v):
        x1_ref[new] = v

    mlp = _mlp_pieces(lambda: x1_ref[old], fpre_ref, fpost_ref, w1_ref, w2_ref, out_ref)
    rec = _recur_pieces(x_ref, qk_ref, v_ref, o_ref, g_ref, cw_ref, cb_ref, gpost_ref, hnw_ref,
                        wout_ref, hist_ref, q_ref, k_ref, c_ref, n_ref, m_ref, h_ref, store)
    _interleave(mlp, rec)


def _mlstm_mlp_layer(x2, qk, v, o, g, conv_w, conv_b, gpost, hnw, w_out, fpre, fpost, w1, w2,
                     *, seq_len, ts=ROW_TILE):
    t, d = x2.shape
    dh = d // N_HEADS
    assert dh == MLSTM_CHUNK == LANES, "score and head-output tiles share one lane layout"
    n_tiles = t // ts
    tile = lambda width: pl.BlockSpec((ts, width), lambda i: (jnp.minimum(i, n_tiles - 1), 0))
    return pl.pallas_call(
        functools.partial(_mlstm_mlp_kernel, tiles_per_seq=seq_len // ts),
        grid=(n_tiles + 1,),
        in_specs=[
            tile(d), tile(2 * d), tile(d), tile(d), tile(2 * LANES),
            _resident(conv_w.shape), _resident(conv_b.shape),
            _resident((1, d)), _resident((1, d)), _resident(w_out.shape),
            _resident((1, d)), _resident((1, d)), _resident(w1.shape), _resident(w2.shape),
        ],
        out_specs=pl.BlockSpec((ts, d), lambda i: (jnp.maximum(i - 1, 0), 0)),
        out_shape=jax.ShapeDtypeStruct((t, d), F32),
        scratch_shapes=[
            pltpu.VMEM((CONV_HALO, 2 * d), F32),
            pltpu.VMEM((ts, d), BF16),
            pltpu.VMEM((ts, d), F32),
            pltpu.VMEM((N_HEADS, dh, dh), F32),
            pltpu.VMEM((N_HEADS, dh), F32),
            pltpu.VMEM((1, LANES), F32),
            pltpu.VMEM((ts, d), F32),
            pltpu.VMEM((2, ts, d), F32),
        ],
        compiler_params=pltpu.CompilerParams(
            dimension_semantics=("arbitrary",), vmem_limit_bytes=VMEM_LIMIT_BYTES),
        name="mlstm_mlp_layer",
    )(x2, qk, v, o, g, conv_w, conv_b, gpost, hnw, w_out, fpre, fpost, w1, w2)


def _pad_gates(w_in, i_bias, f_bias, d):
    h = N_HEADS
    wg = jnp.zeros((d, 2 * LANES), F32)
    wg = wg.at[:, :h].set(w_in[:, 4 * d:4 * d + h])
    wg = wg.at[:, LANES:LANES + h].set(w_in[:, 4 * d + h:])
    gb = jnp.zeros((1, 2 * LANES), F32)
    gb = gb.at[0, :h].set(i_bias)
    gb = gb.at[0, LANES:LANES + h].set(f_bias)
    return wg.astype(BF16), gb


def kernel(x, mix_pre_g, mix_post_g, ffn_pre_g, ffn_post_g, pool_w, pool_b, pool_scale,
           mlstm_w_in, mlstm_conv_w, mlstm_conv_b, mlstm_i_bias, mlstm_f_bias,
           mlstm_head_norm_w, mlstm_w_out, mlp_w1, mlp_w2):
    bsz, s, d = x.shape
    depth = mix_pre_g.shape[0]
    d_ff = mlp_w1.shape[2]
    row = lambda a: a.reshape(1, -1)
    n_groups, cg = pool_w.shape[1], pool_w.shape[2]
    pool_w_rows = pool_w.reshape(pool_w.shape[0], n_groups * cg, cg)
    mlp16 = {0: (mlp_w1[0].astype(BF16), mlp_w2[0].astype(BF16))}
    pool16 = {0: pool_w_rows[0].astype(BF16)}
    mlstm16 = {}
    x2 = x.reshape(bsz * s, d)
    for i in range(depth):
        j = i // 2
        nxt = i + 1 < depth
        mlp_casts = [(mlp_w1, i + 1, d_ff), (mlp_w2, i + 1, d)] if nxt else []
        mlp_args = (row(ffn_pre_g[i]), row(ffn_post_g[i])) + mlp16[i]
        if i % 2 == 0:
            casts = ([(mlstm_w_in, j, 4 * d), (mlstm_w_out, j, d)] if nxt else []) + mlp_casts
            x2, cast = _pool_mlp_layer(
                x2, row(mix_pre_g[i]), row(mix_post_g[i]), pool16[j].reshape(n_groups, cg, cg),
                row(pool_b[j]), row(pool_scale[j]), *mlp_args, seq_len=s, casts=casts)
            if nxt:
                mlstm16[j], mlp16[i + 1] = (cast[0], cast[1]), (cast[2], cast[3])
        else:
            w_in16, w_out16 = mlstm16[j]
            w_gate, gate_bias = _pad_gates(mlstm_w_in[j], mlstm_i_bias[j], mlstm_f_bias[j], d)
            casts = ([(pool_w_rows, j + 1, cg)] + mlp_casts) if nxt else []
            (qk, v, o, g), cast = _mlstm_proj_layer(
                x2, row(mix_pre_g[i]), w_in16, w_gate, gate_bias, casts=casts)
            if nxt:
                pool16[j + 1], mlp16[i + 1] = cast[0], (cast[1], cast[2])
            x2 = _mlstm_mlp_layer(x2, qk, v, o, g, mlstm_conv_w[j], row(mlstm_conv_b[j]),
                                  row(mix_post_g[i]), row(mlstm_head_norm_w[j]), w_out16,
                                  *mlp_args, seq_len=s)
    return x2.reshape(bsz, s, d)
```

```python
import functools

import jax
import jax.numpy as jnp
from jax import lax
from jax.experimental import pallas as pl
from jax.experimental.pallas import tpu as pltpu

F32 = jnp.float32
BF16 = jnp.bfloat16

RMS_EPS = 1e-6
POOL_WINDOWS = (2, 4, 8, 16)
N_HEADS = 8
CONV_WIDTH = 4

LANES = 128
SUBLANES = 8
VMEM_LIMIT_BYTES = 60 * 1024 * 1024

POOL_HALO = 16
CONV_HALO = SUBLANES
MLSTM_CHUNK = 128
ROW_TILE = 512
MLP_CHUNK = 512
PROJ_BLOCK = 256
CAST_CHUNKS = 64


def _rms(x, g):
    ms = jnp.mean(x * x, axis=-1, keepdims=True)
    return x * lax.rsqrt(ms + RMS_EPS) * g


def _resident(shape):
    nd = len(shape)
    return pl.BlockSpec(shape, lambda *_: (0,) * nd, pipeline_mode=pl.Buffered(1))


def _interleave(*streams):
    n = max(len(s) for s in streams)
    pos = [0] * len(streams)
    for i in range(n):
        for k, s in enumerate(streams):
            upto = -(-(i + 1) * len(s) // n)
            while pos[k] < upto:
                s[pos[k]]()
                pos[k] += 1


def _with_casts(body, n_in, n_out, n_cast):
    def kernel(*refs):
        a, b, c = n_in + n_cast, n_in + n_cast + n_out, n_in + 2 * n_cast + n_out
        for src, dst in zip(refs[n_in:a], refs[b:c]):
            dst[...] = src[:, :dst.shape[1]].astype(BF16)
        body(*refs[:n_in], *refs[a:b], *refs[c:])
    return kernel


def _cast_specs(casts, n_steps):
    in_specs, out_specs, out_shapes, args = [], [], [], []
    for stacked, layer, cols_out in casts:
        _, rows, cols = stacked.shape
        chunk = rows // CAST_CHUNKS
        pick = lambda i: jnp.minimum(i, CAST_CHUNKS - 1)
        assert n_steps >= CAST_CHUNKS and chunk * CAST_CHUNKS == rows and chunk % 16 == 0
        in_specs.append(pl.BlockSpec((None, chunk, cols), lambda i, l=layer: (l, pick(i), 0)))
        out_specs.append(pl.BlockSpec((chunk, cols_out), lambda i: (pick(i), 0)))
        out_shapes.append(jax.ShapeDtypeStruct((rows, cols_out), BF16))
        args.append(stacked)
    return in_specs, out_specs, out_shapes, args


def _mlp_pieces(load_x, gpre_ref, gpost_ref, w1_ref, w2_ref, out_ref):
    st = {}
    d_ff = w1_ref.shape[1]

    def head():
        st["hn"] = _rms(load_x(), gpre_ref[...]).astype(BF16)
        st["acc"] = None

    def chunk(c):
        def run():
            cols = slice(c * MLP_CHUNK, (c + 1) * MLP_CHUNK)
            h = jnp.dot(st["hn"], w1_ref[:, cols], preferred_element_type=F32)
            a = jnp.maximum(h, 0.0)
            a = (a * a).astype(BF16)
            p = jnp.dot(a, w2_ref[cols, :], preferred_element_type=F32)
            st["acc"] = p if st["acc"] is None else st["acc"] + p
        return run

    def tail():
        out_ref[...] = load_x() + _rms(st["acc"], gpost_ref[...])

    return [head] + [chunk(c) for c in range(d_ff // MLP_CHUNK)] + [tail]


def _skew_slots(step, x1_ref):
    new = step % 2

    @pl.when(step == 0)
    def _():
        x1_ref[1] = jnp.zeros(x1_ref.shape[1:], F32)

    return 1 - new, new


def _pool_pieces(x_ref, gpre_ref, gpost_ref, w_ref, b_ref, scale_ref, halo_ref, store, first_row):
    st = {"parts": []}
    ts, d = x_ref.shape
    cg = d // len(POOL_WINDOWS)

    def head():
        x = x_ref[...]
        hn = _rms(x, gpre_ref[...])
        st["hn"] = hn
        st["ext"] = jnp.concatenate([halo_ref[...], hn], axis=0)
        halo_ref[...] = hn[ts - POOL_HALO:, :]
        st["t"] = (first_row + lax.broadcasted_iota(jnp.int32, (ts, 1), 0)).astype(F32)

    def group(g, w):
        def run():
            e = st["ext"][:, g * cg:(g + 1) * cg]
            sh = 1
            while sh < w:
                e = e + pltpu.roll(e, sh, axis=0)
                sh *= 2
            inv_cnt = 1.0 / jnp.minimum(st["t"] + 1.0, float(w))
            pooled = e[POOL_HALO:, :] * inv_cnt - st["hn"][:, g * cg:(g + 1) * cg]
            st["parts"].append(jnp.dot(pooled.astype(BF16), w_ref[g], preferred_element_type=F32))
        return run

    def tail():
        y = jnp.concatenate(st["parts"], axis=-1)
        y = (y + b_ref[...]) * scale_ref[...]
        store(x_ref[...] + _rms(y, gpost_ref[...]))

    return [head] + [group(g, w) for g, w in enumerate(POOL_WINDOWS)] + [tail]


def _pool_mlp_kernel(x_ref, gpre_ref, gpost_ref, w_ref, b_ref, scale_ref,
                     fpre_ref, fpost_ref, w1_ref, w2_ref, out_ref, halo_ref, x1_ref,
                     *, tiles_per_seq):
    step = pl.program_id(0)
    ts = x_ref.shape[0]
    old, new = _skew_slots(step, x1_ref)

    @pl.when(step % tiles_per_seq == 0)
    def _():
        halo_ref[...] = jnp.zeros_like(halo_ref)

    def store(v):
        x1_ref[new] = v

    mlp = _mlp_pieces(lambda: x1_ref[old], fpre_ref, fpost_ref, w1_ref, w2_ref, out_ref)
    pool = _pool_pieces(x_ref, gpre_ref, gpost_ref, w_ref, b_ref, scale_ref, halo_ref, store,
                        (step % tiles_per_seq) * ts)
    _interleave(mlp, pool)


def _pool_mlp_layer(x2, gpre, gpost, w, b, scale, fpre, fpost, w1, w2, *, seq_len, casts=(),
                    ts=ROW_TILE):
    t, d = x2.shape
    n_tiles = t // ts
    c_in, c_out, c_shapes, c_args = _cast_specs(casts, n_tiles + 1)
    outs = pl.pallas_call(
        _with_casts(functools.partial(_pool_mlp_kernel, tiles_per_seq=seq_len // ts),
                    10, 1, len(casts)),
        grid=(n_tiles + 1,),
        in_specs=[
            pl.BlockSpec((ts, d), lambda i: (jnp.minimum(i, n_tiles - 1), 0)),
            _resident((1, d)), _resident((1, d)), _resident(w.shape),
            _resident((1, d)), _resident((1, d)),
            _resident((1, d)), _resident((1, d)), _resident(w1.shape), _resident(w2.shape),
        ] + c_in,
        out_specs=[pl.BlockSpec((ts, d), lambda i: (jnp.maximum(i - 1, 0), 0))] + c_out,
        out_shape=[jax.ShapeDtypeStruct((t, d), F32)] + c_shapes,
        scratch_shapes=[
            pltpu.VMEM((POOL_HALO, d), F32),
            pltpu.VMEM((2, ts, d), F32),
        ],
        compiler_params=pltpu.CompilerParams(
            dimension_semantics=("arbitrary",), vmem_limit_bytes=VMEM_LIMIT_BYTES),
        name="pool_mlp_layer",
    )(x2, gpre, gpost, w, b, scale, fpre, fpost, w1, w2, *c_args)
    return outs[0], outs[1:]


def _mlstm_proj_kernel(x_ref, gpre_ref, w_ref, wg_ref, gb_ref, qk_ref, v_ref, o_ref, g_ref):
    d = x_ref.shape[1]
    hn = _rms(x_ref[...], gpre_ref[...]).astype(BF16)
    for blk in range(4 * d // PROJ_BLOCK):
        part, off = divmod(blk * PROJ_BLOCK, d)
        cols = slice(blk * PROJ_BLOCK, (blk + 1) * PROJ_BLOCK)
        oc = slice(off, off + PROJ_BLOCK)
        pre = jnp.dot(hn, w_ref[:, cols], preferred_element_type=F32)
        if part < 2:
            qk_ref[:, cols] = pre
        elif part == 2:
            v_ref[:, oc] = pre.astype(BF16)
        else:
            o_ref[:, oc] = pre
    g_ref[...] = jnp.dot(hn, wg_ref[...], preferred_element_type=F32) + gb_ref[...]


def _mlstm_proj_layer(x2, gpre, w_main, w_gate, gate_bias, *, casts=(), ts=ROW_TILE):
    t, d = x2.shape
    tile = lambda width: pl.BlockSpec((ts, width), lambda i: (i, 0))
    c_in, c_out, c_shapes, c_args = _cast_specs(casts, t // ts)
    outs = pl.pallas_call(
        _with_casts(_mlstm_proj_kernel, 5, 4, len(casts)),
        grid=(t // ts,),
        in_specs=[tile(d), _resident((1, d)), _resident(w_main.shape), _resident(w_gate.shape),
                  _resident(gate_bias.shape)] + c_in,
        out_specs=[tile(2 * d), tile(d), tile(d), tile(2 * LANES)] + c_out,
        out_shape=[
            jax.ShapeDtypeStruct((t, 2 * d), F32),
            jax.ShapeDtypeStruct((t, d), BF16),
            jax.ShapeDtypeStruct((t, d), F32),
            jax.ShapeDtypeStruct((t, 2 * LANES), F32),
        ] + c_shapes,
        compiler_params=pltpu.CompilerParams(
            dimension_semantics=("arbitrary",), vmem_limit_bytes=VMEM_LIMIT_BYTES),
        name="mlstm_proj_layer",
    )(x2, gpre, w_main, w_gate, gate_bias, *c_args)
    return outs[:4], outs[4:]


def _scan_rows(x, op, fill):
    n = x.shape[0]
    row = lax.broadcasted_iota(jnp.int32, x.shape, 0)
    sh = 1
    while sh < n:
        x = op(x, jnp.where(row >= sh, pltpu.roll(x, sh, axis=0), fill))
        sh *= 2
    return x


def _log_sigmoid(x):
    return -(jnp.maximum(-x, 0.0) + jnp.log1p(jnp.exp(-jnp.abs(x))))


def _chunk_gates(g_ref, m_ref, rows):
    chunk = rows.stop - rows.start
    ig = g_ref[rows, :LANES]
    b = _scan_rows(_log_sigmoid(g_ref[rows, LANES:]), jnp.add, 0.0)
    r = ig - b
    pm = _scan_rows(r, jnp.maximum, -jnp.inf)
    m_prev = m_ref[...]
    u = jnp.maximum(pm, m_prev)
    g = b[chunk - 1:chunk, :]
    a = g + r
    m_new = jnp.maximum(g + m_prev, jnp.max(a, axis=0, keepdims=True))
    m_ref[...] = m_new
    return dict(
        u=u,
        m_prev=m_prev,
        inv_floor=jnp.exp(-(b + u)),
        decay=jnp.exp(g + m_prev - m_new),
        wts=jnp.exp(a - m_new),
        r_t=r.T,
    )


def _head_scores(h, rows, gt, causal, q_ref, k_ref):
    dh = q_ref.shape[-1] // N_HEADS
    cs = slice(h * dh, (h + 1) * dh)
    hl = slice(h, h + 1)
    s = lax.dot_general(q_ref[rows, cs], k_ref[rows, cs].astype(BF16), (((1,), (1,)), ((), ())),
                        preferred_element_type=F32)
    u_bc = jnp.broadcast_to(gt["u"][:, hl], s.shape)
    arg = jnp.where(causal, gt["r_t"][hl, :] - u_bc, -jnp.inf)
    return s * jnp.exp(arg), jnp.exp(gt["m_prev"][:, hl] - u_bc)


def _head_finish(h, rows, gt, scored, q_ref, k_ref, v_ref, c_ref, n_ref, h_ref):
    s, inter = scored
    dh = q_ref.shape[-1] // N_HEADS
    cs = slice(h * dh, (h + 1) * dh)
    hl = slice(h, h + 1)
    qh = q_ref[rows, cs]
    kf = k_ref[rows, cs]
    vh = v_ref[rows, cs]
    c_old = c_ref[h]
    n_old = n_ref[hl, :]
    num = (jnp.dot(s.astype(BF16), vh, preferred_element_type=F32)
           + inter * jnp.dot(qh, c_old.astype(BF16), preferred_element_type=F32))
    qn = qh.astype(F32) * n_old.astype(BF16).astype(F32)
    den = jnp.sum(s + inter * qn, axis=-1, keepdims=True)
    rinv = 1.0 / jnp.maximum(jnp.abs(den), gt["inv_floor"][:, hl])
    ms = jnp.mean(num * num, axis=-1, keepdims=True)
    h_ref[rows, cs] = num * (rinv * lax.rsqrt(rinv * rinv * ms + RMS_EPS))
    kw = kf * gt["wts"][:, hl]
    decay_h = gt["decay"][:, hl]
    c_ref[h] = decay_h * c_old + lax.dot_general(
        kw.astype(BF16), vh, (((0,), (0,)), ((), ())), preferred_element_type=F32)
    n_ref[hl, :] = decay_h * n_old + jnp.sum(kw, axis=0, keepdims=True)


def _conv_block(blk, qk_ref, cw_ref, cb_ref, hist_ref, q_ref, k_ref):
    ts, d = q_ref.shape
    dh = d // N_HEADS
    part, off = divmod(blk * PROJ_BLOCK, d)
    cols = slice(blk * PROJ_BLOCK, (blk + 1) * PROJ_BLOCK)
    oc = slice(off, off + PROJ_BLOCK)
    pre = qk_ref[:, cols]
    ext = jnp.concatenate([hist_ref[:, cols], pre], axis=0)
    hist_ref[:, cols] = pre[ts - CONV_HALO:, :]
    assert CONV_WIDTH == 4
    w = [cw_ref[tap:tap + 1, cols] for tap in range(CONV_WIDTH)]
    ext1 = pltpu.roll(ext, 1, axis=0)
    far = pltpu.roll(w[1] * ext + w[0] * ext1, 2, axis=0)
    conv = cb_ref[:, cols] + far[CONV_HALO:, :] + (w[3] * ext + w[2] * ext1)[CONV_HALO:, :]
    act = conv * jax.nn.sigmoid(conv)
    if part == 0:
        q_ref[:, oc] = act.astype(BF16)
    else:
        k_ref[:, oc] = act * (dh ** -0.5)


def _recur_pieces(x_ref, qk_ref, v_ref, o_ref, g_ref, cw_ref, cb_ref, gpost_ref, hnw_ref, wout_ref,
                  hist_ref, q_ref, k_ref, c_ref, n_ref, m_ref, h_ref, store):
    ts, d = x_ref.shape
    chunk = MLSTM_CHUNK
    causal = (lax.broadcasted_iota(jnp.int32, (chunk, chunk), 0)
              >= lax.broadcasted_iota(jnp.int32, (chunk, chunk), 1))
    units = [(c, h) for c in range(ts // chunk) for h in range(N_HEADS)]
    rows_of = lambda c: slice(c * chunk, (c + 1) * chunk)
    gates, scores = {}, {}

    def prepare(u):
        if u >= len(units):
            return
        c, h = units[u]
        if c not in gates:
            gates[c] = _chunk_gates(g_ref, m_ref, rows_of(c))
        scores[u] = _head_scores(h, rows_of(c), gates[c], causal, q_ref, k_ref)

    def unit(u):
        def run():
            prepare(u + 1)
            c, h = units[u]
            _head_finish(h, rows_of(c), gates[c], scores.pop(u), q_ref, k_ref, v_ref,
                         c_ref, n_ref, h_ref)
        return run

    def tail():
        gated = jax.nn.sigmoid(o_ref[...]) * (h_ref[...] * hnw_ref[...])
        y = jnp.dot(gated.astype(BF16), wout_ref[...], preferred_element_type=F32)
        store(x_ref[...] + _rms(y, gpost_ref[...]))

    conv = [functools.partial(_conv_block, blk, qk_ref, cw_ref, cb_ref, hist_ref, q_ref, k_ref)
            for blk in range(2 * d // PROJ_BLOCK)]
    return conv + [lambda: prepare(0)] + [unit(u) for u in range(len(units))] + [tail]


def _mlstm_mlp_kernel(x_ref, qk_ref, v_ref, o_ref, g_ref, cw_ref, cb_ref, gpost_ref, hnw_ref,
                      wout_ref, fpre_ref, fpost_ref, w1_ref, w2_ref, out_ref,
                      hist_ref, q_ref, k_ref, c_ref, n_ref, m_ref, h_ref, x1_ref, *, tiles_per_seq):
    step = pl.program_id(0)
    old, new = _skew_slots(step, x1_ref)

    @pl.when(step % tiles_per_seq == 0)
    def _():
        hist_ref[...] = jnp.zeros_like(hist_ref)
        c_ref[...] = jnp.zeros_like(c_ref)
        n_ref[...] = jnp.zeros_like(n_ref)
        m_ref[...] = jnp.zeros_like(m_ref)

    def store(v):
        x1_ref[new] = v

    mlp = _mlp_pieces(lambda: x1_ref[old], fpre_ref, fpost_ref, w1_ref, w2_ref, out_ref)
    rec = _recur_pieces(x_ref, qk_ref, v_ref, o_ref, g_ref, cw_ref, cb_ref, gpost_ref, hnw_ref,
                        wout_ref, hist_ref, q_ref, k_ref, c_ref, n_ref, m_ref, h_ref, store)
    _interleave(mlp, rec)


def _mlstm_mlp_layer(x2, qk, v, o, g, conv_w, conv_b, gpost, hnw, w_out, fpre, fpost, w1, w2,
                     *, seq_len, ts=ROW_TILE):
    t, d = x2.shape
    dh = d // N_HEADS
    assert dh == MLSTM_CHUNK == LANES, "score and head-output tiles share one lane layout"
    n_tiles = t // ts
    tile = lambda width: pl.BlockSpec((ts, width), lambda i: (jnp.minimum(i, n_tiles - 1), 0))
    return pl.pallas_call(
        functools.partial(_mlstm_mlp_kernel, tiles_per_seq=seq_len // ts),
        grid=(n_tiles + 1,),
        in_specs=[
            tile(d), tile(2 * d), tile(d), tile(d), tile(2 * LANES),
            _resident(conv_w.shape), _resident(conv_b.shape),
            _resident((1, d)), _resident((1, d)), _resident(w_out.shape),
            _resident((1, d)), _resident((1, d)), _resident(w1.shape), _resident(w2.shape),
        ],
        out_specs=pl.BlockSpec((ts, d), lambda i: (jnp.maximum(i - 1, 0), 0)),
        out_shape=jax.ShapeDtypeStruct((t, d), F32),
        scratch_shapes=[
            pltpu.VMEM((CONV_HALO, 2 * d), F32),
            pltpu.VMEM((ts, d), BF16),
            pltpu.VMEM((ts, d), F32),
            pltpu.VMEM((N_HEADS, dh, dh), F32),
            pltpu.VMEM((N_HEADS, dh), F32),
            pltpu.VMEM((1, LANES), F32),
            pltpu.VMEM((ts, d), F32),
            pltpu.VMEM((2, ts, d), F32),
        ],
        compiler_params=pltpu.CompilerParams(
            dimension_semantics=("arbitrary",), vmem_limit_bytes=VMEM_LIMIT_BYTES),
        name="mlstm_mlp_layer",
    )(x2, qk, v, o, g, conv_w, conv_b, gpost, hnw, w_out, fpre, fpost, w1, w2)


def _pad_gates(w_in, i_bias, f_bias, d):
    h = N_HEADS
    wg = jnp.zeros((d, 2 * LANES), F32)
    wg = wg.at[:, :h].set(w_in[:, 4 * d:4 * d + h])
    wg = wg.at[:, LANES:LANES + h].set(w_in[:, 4 * d + h:])
    gb = jnp.zeros((1, 2 * LANES), F32)
    gb = gb.at[0, :h].set(i_bias)
    gb = gb.at[0, LANES:LANES + h].set(f_bias)
    return wg.astype(BF16), gb


def kernel(x, mix_pre_g, mix_post_g, ffn_pre_g, ffn_post_g, pool_w, pool_b, pool_scale,
           mlstm_w_in, mlstm_conv_w, mlstm_conv_b, mlstm_i_bias, mlstm_f_bias,
           mlstm_head_norm_w, mlstm_w_out, mlp_w1, mlp_w2):
    bsz, s, d = x.shape
    depth = mix_pre_g.shape[0]
    d_ff = mlp_w1.shape[2]
    row = lambda a: a.reshape(1, -1)
    n_groups, cg = pool_w.shape[1], pool_w.shape[2]
    pool_w_rows = pool_w.reshape(pool_w.shape[0], n_groups * cg, cg)
    mlp16 = {0: (mlp_w1[0].astype(BF16), mlp_w2[0].astype(BF16))}
    pool16 = {0: pool_w_rows[0].astype(BF16)}
    mlstm16 = {}
    x2 = x.reshape(bsz * s, d)
    for i in range(depth):
        j = i // 2
        nxt = i + 1 < depth
        mlp_casts = [(mlp_w1, i + 1, d_ff), (mlp_w2, i + 1, d)] if nxt else []
        mlp_args = (row(ffn_pre_g[i]), row(ffn_post_g[i])) + mlp16[i]
        if i % 2 == 0:
            casts = ([(mlstm_w_in, j, 4 * d), (mlstm_w_out, j, d)] if nxt else []) + mlp_casts
            x2, cast = _pool_mlp_layer(
                x2, row(mix_pre_g[i]), row(mix_post_g[i]), pool16[j].reshape(n_groups, cg, cg),
                row(pool_b[j]), row(pool_scale[j]), *mlp_args, seq_len=s, casts=casts)
            if nxt:
                mlstm16[j], mlp16[i + 1] = (cast[0], cast[1]), (cast[2], cast[3])
        else:
            w_in16, w_out16 = mlstm16[j]
            w_gate, gate_bias = _pad_gates(mlstm_w_in[j], mlstm_i_bias[j], mlstm_f_bias[j], d)
            casts = ([(pool_w_rows, j + 1, cg)] + mlp_casts) if nxt else []
            (qk, v, o, g), cast = _mlstm_proj_layer(
                x2, row(mix_pre_g[i]), w_in16, w_gate, gate_bias, casts=casts)
            if nxt:
                pool16[j + 1], mlp16[i + 1] = cast[0], (cast[1], cast[2])
            x2 = _mlstm_mlp_layer(x2, qk, v, o, g, mlstm_conv_w[j], row(mlstm_conv_b[j]),
                                  row(mix_post_g[i]), row(mlstm_head_norm_w[j]), w_out16,
                                  *mlp_args, seq_len=s)
    return x2.reshape(bsz, s, d)
```

```python
import functools

import jax
import jax.numpy as jnp
from jax import lax
from jax.experimental import pallas as pl
from jax.experimental.pallas import tpu as pltpu

F32 = jnp.float32
BF16 = jnp.bfloat16

RMS_EPS = 1e-6
POOL_WINDOWS = (2, 4, 8, 16)
N_HEADS = 8
CONV_WIDTH = 4

LANES = 128
SUBLANES = 8
VMEM_LIMIT_BYTES = 60 * 1024 * 1024

POOL_HALO = 16
CONV_HALO = SUBLANES
MLSTM_CHUNK = 128
ROW_TILE = 512
MLP_CHUNK = 512
PROJ_BLOCK = 256
CAST_CHUNKS = 64


def _rms(x, g):
    ms = jnp.mean(x * x, axis=-1, keepdims=True)
    return x * lax.rsqrt(ms + RMS_EPS) * g


def _resident(shape):
    nd = len(shape)
    return pl.BlockSpec(shape, lambda *_: (0,) * nd, pipeline_mode=pl.Buffered(1))


def _interleave(*streams):
    n = max(len(s) for s in streams)
    pos = [0] * len(streams)
    for i in range(n):
        for k, s in enumerate(streams):
            upto = -(-(i + 1) * len(s) // n)
            while pos[k] < upto:
                s[pos[k]]()
                pos[k] += 1


def _with_casts(body, n_in, n_out, n_cast):
    def kernel(*refs):
        a, b, c = n_in + n_cast, n_in + n_cast + n_out, n_in + 2 * n_cast + n_out
        for src, dst in zip(refs[n_in:a], refs[b:c]):
            dst[...] = src[:, :dst.shape[1]].astype(BF16)
        body(*refs[:n_in], *refs[a:b], *refs[c:])
    return kernel


def _cast_specs(casts, n_steps):
    in_specs, out_specs, out_shapes, args = [], [], [], []
    for stacked, layer, cols_out in casts:
        _, rows, cols = stacked.shape
        n_chunks = min(CAST_CHUNKS, n_steps)
        chunk = rows // n_chunks
        pick = lambda i, n=n_chunks: jnp.minimum(i, n - 1)
        assert chunk * n_chunks == rows and chunk % 16 == 0
        in_specs.append(pl.BlockSpec((None, chunk, cols), lambda i, l=layer: (l, pick(i), 0)))
        out_specs.append(pl.BlockSpec((chunk, cols_out), lambda i: (pick(i), 0)))
        out_shapes.append(jax.ShapeDtypeStruct((rows, cols_out), BF16))
        args.append(stacked)
    return in_specs, out_specs, out_shapes, args


def _mlp_pieces(load_x, gpre_ref, gpost_ref, w1_ref, w2_ref, out_ref):
    st = {}
    d_ff = w1_ref.shape[1]

    def head():
        st["hn"] = _rms(load_x(), gpre_ref[...]).astype(BF16)
        st["acc"] = None

    def chunk(c):
        def run():
            cols = slice(c * MLP_CHUNK, (c + 1) * MLP_CHUNK)
            h = jnp.dot(st["hn"], w1_ref[:, cols], preferred_element_type=F32)
            a = jnp.maximum(h, 0.0)
            a = (a * a).astype(BF16)
            p = jnp.dot(a, w2_ref[cols, :], preferred_element_type=F32)
            st["acc"] = p if st["acc"] is None else st["acc"] + p
        return run

    def tail():
        out_ref[...] = load_x() + _rms(st["acc"], gpost_ref[...])

    return [head] + [chunk(c) for c in range(d_ff // MLP_CHUNK)] + [tail]


def _skew_slots(step, x1_ref):
    new = step % 2

    @pl.when(step == 0)
    def _():
        x1_ref[1] = jnp.zeros(x1_ref.shape[1:], F32)

    return 1 - new, new


def _pool_pieces(x_ref, gpre_ref, gpost_ref, w_ref, b_ref, scale_ref, halo_ref, store, first_row):
    st = {"parts": []}
    ts, d = x_ref.shape
    cg = d // len(POOL_WINDOWS)

    def head():
        x = x_ref[...]
        hn = _rms(x, gpre_ref[...])
        st["hn"] = hn
        st["ext"] = jnp.concatenate([halo_ref[...], hn], axis=0)
        halo_ref[...] = hn[ts - POOL_HALO:, :]
        st["t"] = (first_row + lax.broadcasted_iota(jnp.int32, (ts, 1), 0)).astype(F32)

    def group(g, w):
        def run():
            e = st["ext"][:, g * cg:(g + 1) * cg]
            sh = 1
            while sh < w:
                e = e + pltpu.roll(e, sh, axis=0)
                sh *= 2
            inv_cnt = 1.0 / jnp.minimum(st["t"] + 1.0, float(w))
            pooled = e[POOL_HALO:, :] * inv_cnt - st["hn"][:, g * cg:(g + 1) * cg]
            st["parts"].append(jnp.dot(pooled.astype(BF16), w_ref[g], preferred_element_type=F32))
        return run

    def tail():
        y = jnp.concatenate(st["parts"], axis=-1)
        y = (y + b_ref[...]) * scale_ref[...]
        store(x_ref[...] + _rms(y, gpost_ref[...]))

    return [head] + [group(g, w) for g, w in enumerate(POOL_WINDOWS)] + [tail]


def _pool_mlp_kernel(x_ref, gpre_ref, gpost_ref, w_ref, b_ref, scale_ref,
                     fpre_ref, fpost_ref, w1_ref, w2_ref, out_ref, halo_ref, x1_ref,
                     *, tiles_per_seq):
    step = pl.program_id(0)
    ts = x_ref.shape[0]
    old, new = _skew_slots(step, x1_ref)

    @pl.when(step % tiles_per_seq == 0)
    def _():
        halo_ref[...] = jnp.zeros_like(halo_ref)

    def store(v):
        x1_ref[new] = v

    mlp = _mlp_pieces(lambda: x1_ref[old], fpre_ref, fpost_ref, w1_ref, w2_ref, out_ref)
    pool = _pool_pieces(x_ref, gpre_ref, gpost_ref, w_ref, b_ref, scale_ref, halo_ref, store,
                        (step % tiles_per_seq) * ts)
    _interleave(mlp, pool)


def _pool_mlp_layer(x2, gpre, gpost, w, b, scale, fpre, fpost, w1, w2, *, seq_len, casts=(),
                    ts=ROW_TILE):
    t, d = x2.shape
    n_tiles = t // ts
    c_in, c_out, c_shapes, c_args = _cast_specs(casts, n_tiles + 1)
    outs = pl.pallas_call(
        _with_casts(functools.partial(_pool_mlp_kernel, tiles_per_seq=seq_len // ts),
                    10, 1, len(casts)),
        grid=(n_tiles + 1,),
        in_specs=[
            pl.BlockSpec((ts, d), lambda i: (jnp.minimum(i, n_tiles - 1), 0)),
            _resident((1, d)), _resident((1, d)), _resident(w.shape),
            _resident((1, d)), _resident((1, d)),
            _resident((1, d)), _resident((1, d)), _resident(w1.shape), _resident(w2.shape),
        ] + c_in,
        out_specs=[pl.BlockSpec((ts, d), lambda i: (jnp.maximum(i - 1, 0), 0))] + c_out,
        out_shape=[jax.ShapeDtypeStruct((t, d), F32)] + c_shapes,
        scratch_shapes=[
            pltpu.VMEM((POOL_HALO, d), F32),
            pltpu.VMEM((2, ts, d), F32),
        ],
        compiler_params=pltpu.CompilerParams(
            dimension_semantics=("arbitrary",), vmem_limit_bytes=VMEM_LIMIT_BYTES),
        name="pool_mlp_layer",
    )(x2, gpre, gpost, w, b, scale, fpre, fpost, w1, w2, *c_args)
    return outs[0], outs[1:]


def _mlstm_proj_kernel(x_ref, gpre_ref, w_ref, wg_ref, gb_ref, qk_ref, v_ref, o_ref, g_ref):
    d = x_ref.shape[1]
    hn = _rms(x_ref[...], gpre_ref[...]).astype(BF16)
    for blk in range(4 * d // PROJ_BLOCK):
        part, off = divmod(blk * PROJ_BLOCK, d)
        cols = slice(blk * PROJ_BLOCK, (blk + 1) * PROJ_BLOCK)
        oc = slice(off, off + PROJ_BLOCK)
        pre = jnp.dot(hn, w_ref[:, cols], preferred_element_type=F32)
        if part < 2:
            qk_ref[:, cols] = pre
        elif part == 2:
            v_ref[:, oc] = pre.astype(BF16)
        else:
            o_ref[:, oc] = jax.nn.sigmoid(pre)
    gates = jnp.dot(hn, wg_ref[...], preferred_element_type=F32) + gb_ref[...]
    g_ref[:, :LANES] = gates[:, :LANES]
    g_ref[:, LANES:] = _log_sigmoid(gates[:, LANES:])


def _mlstm_proj_layer(x2, gpre, w_main, w_gate, gate_bias, *, casts=(), ts=2 * ROW_TILE):
    t, d = x2.shape
    tile = lambda width: pl.BlockSpec((ts, width), lambda i: (i, 0))
    c_in, c_out, c_shapes, c_args = _cast_specs(casts, t // ts)
    outs = pl.pallas_call(
        _with_casts(_mlstm_proj_kernel, 5, 4, len(casts)),
        grid=(t // ts,),
        in_specs=[tile(d), _resident((1, d)), _resident(w_main.shape), _resident(w_gate.shape),
                  _resident(gate_bias.shape)] + c_in,
        out_specs=[tile(2 * d), tile(d), tile(d), tile(2 * LANES)] + c_out,
        out_shape=[
            jax.ShapeDtypeStruct((t, 2 * d), F32),
            jax.ShapeDtypeStruct((t, d), BF16),
            jax.ShapeDtypeStruct((t, d), F32),
            jax.ShapeDtypeStruct((t, 2 * LANES), F32),
        ] + c_shapes,
        compiler_params=pltpu.CompilerParams(
            dimension_semantics=("arbitrary",), vmem_limit_bytes=VMEM_LIMIT_BYTES),
        name="mlstm_proj_layer",
    )(x2, gpre, w_main, w_gate, gate_bias, *c_args)
    return outs[:4], outs[4:]


def _scan_rows(x, op, fill):
    n = x.shape[0]
    row = lax.broadcasted_iota(jnp.int32, x.shape, 0)
    sh = 1
    while sh < n:
        x = op(x, jnp.where(row >= sh, pltpu.roll(x, sh, axis=0), fill))
        sh *= 2
    return x


def _log_sigmoid(x):
    return -(jnp.maximum(-x, 0.0) + jnp.log1p(jnp.exp(-jnp.abs(x))))


def _chunk_gates(g_ref, m_ref, rows):
    chunk = rows.stop - rows.start
    ig = g_ref[rows, :LANES]
    b = _scan_rows(g_ref[rows, LANES:], jnp.add, 0.0)
    r = ig - b
    pm = _scan_rows(r, jnp.maximum, -jnp.inf)
    m_prev = m_ref[...]
    u = jnp.maximum(pm, m_prev)
    g = b[chunk - 1:chunk, :]
    a = g + r
    m_new = jnp.maximum(g + m_prev, jnp.max(a, axis=0, keepdims=True))
    m_ref[...] = m_new
    return dict(
        u=u,
        m_prev=m_prev,
        inv_floor=jnp.exp(-(b + u)),
        decay=jnp.exp(g + m_prev - m_new),
        wts=jnp.exp(a - m_new),
        r_t=r.T,
    )


def _head_scores(h, rows, gt, causal, q_ref, k_ref):
    dh = q_ref.shape[-1] // N_HEADS
    cs = slice(h * dh, (h + 1) * dh)
    hl = slice(h, h + 1)
    s = lax.dot_general(q_ref[rows, cs], k_ref[rows, cs].astype(BF16), (((1,), (1,)), ((), ())),
                        preferred_element_type=F32)
    u_bc = jnp.broadcast_to(gt["u"][:, hl], s.shape)
    arg = jnp.where(causal, gt["r_t"][hl, :] - u_bc, -jnp.inf)
    return s * jnp.exp(arg), jnp.exp(gt["m_prev"][:, hl] - u_bc)


def _head_finish(h, rows, gt, scored, q_ref, k_ref, v_ref, c_ref, n_ref, h_ref):
    s, inter = scored
    dh = q_ref.shape[-1] // N_HEADS
    cs = slice(h * dh, (h + 1) * dh)
    hl = slice(h, h + 1)
    qh = q_ref[rows, cs]
    kf = k_ref[rows, cs]
    vh = v_ref[rows, cs]
    c_old = c_ref[h]
    n_old = n_ref[hl, :]
    num = (jnp.dot(s.astype(BF16), vh, preferred_element_type=F32)
           + inter * jnp.dot(qh, c_old.astype(BF16), preferred_element_type=F32))
    qn = qh.astype(F32) * n_old.astype(BF16).astype(F32)
    den = jnp.sum(s + inter * qn, axis=-1, keepdims=True)
    rinv = 1.0 / jnp.maximum(jnp.abs(den), gt["inv_floor"][:, hl])
    ms = jnp.mean(num * num, axis=-1, keepdims=True)
    h_ref[rows, cs] = num * (rinv * lax.rsqrt(rinv * rinv * ms + RMS_EPS))
    kw = kf * gt["wts"][:, hl]
    decay_h = gt["decay"][:, hl]
    c_ref[h] = decay_h * c_old + lax.dot_general(
        kw.astype(BF16), vh, (((0,), (0,)), ((), ())), preferred_element_type=F32)
    n_ref[hl, :] = decay_h * n_old + jnp.sum(kw, axis=0, keepdims=True)


def _conv_block(blk, qk_ref, cw_ref, cb_ref, hist_ref, q_ref, k_ref):
    ts, d = q_ref.shape
    dh = d // N_HEADS
    part, off = divmod(blk * PROJ_BLOCK, d)
    cols = slice(blk * PROJ_BLOCK, (blk + 1) * PROJ_BLOCK)
    oc = slice(off, off + PROJ_BLOCK)
    pre = qk_ref[:, cols]
    ext = jnp.concatenate([hist_ref[:, cols], pre], axis=0)
    hist_ref[:, cols] = pre[ts - CONV_HALO:, :]
    assert CONV_WIDTH == 4
    w = [cw_ref[tap:tap + 1, cols] for tap in range(CONV_WIDTH)]
    ext1 = pltpu.roll(ext, 1, axis=0)
    far = pltpu.roll(w[1] * ext + w[0] * ext1, 2, axis=0)
    conv = cb_ref[:, cols] + far[CONV_HALO:, :] + (w[3] * ext + w[2] * ext1)[CONV_HALO:, :]
    act = conv * jax.nn.sigmoid(conv)
    if part == 0:
        q_ref[:, oc] = act.astype(BF16)
    else:
        k_ref[:, oc] = act * (dh ** -0.5)


def _recur_pieces(x_ref, qk_ref, v_ref, o_ref, g_ref, cw_ref, cb_ref, gpost_ref, hnw_ref, wout_ref,
                  hist_ref, q_ref, k_ref, c_ref, n_ref, m_ref, h_ref, store):
    ts, d = x_ref.shape
    chunk = MLSTM_CHUNK
    causal = (lax.broadcasted_iota(jnp.int32, (chunk, chunk), 0)
              >= lax.broadcasted_iota(jnp.int32, (chunk, chunk), 1))
    units = [(c, h) for c in range(ts // chunk) for h in range(N_HEADS)]
    rows_of = lambda c: slice(c * chunk, (c + 1) * chunk)
    gates, scores = {}, {}

    def prepare(u):
        if u >= len(units):
            return
        c, h = units[u]
        if c not in gates:
            gates[c] = _chunk_gates(g_ref, m_ref, rows_of(c))
        scores[u] = _head_scores(h, rows_of(c), gates[c], causal, q_ref, k_ref)

    def unit(u):
        def run():
            prepare(u + 1)
            c, h = units[u]
            _head_finish(h, rows_of(c), gates[c], scores.pop(u), q_ref, k_ref, v_ref,
                         c_ref, n_ref, h_ref)
        return run

    def tail():
        gated = o_ref[...] * (h_ref[...] * hnw_ref[...])
        y = jnp.dot(gated.astype(BF16), wout_ref[...], preferred_element_type=F32)
        store(x_ref[...] + _rms(y, gpost_ref[...]))

    conv = [functools.partial(_conv_block, blk, qk_ref, cw_ref, cb_ref, hist_ref, q_ref, k_ref)
            for blk in range(2 * d // PROJ_BLOCK)]
    return conv + [lambda: prepare(0)] + [unit(u) for u in range(len(units))] + [tail]


def _mlstm_mlp_kernel(x_ref, qk_ref, v_ref, o_ref, g_ref, cw_ref, cb_ref, gpost_ref, hnw_ref,
                      wout_ref, fpre_ref, fpost_ref, w1_ref, w2_ref, out_ref,
                      hist_ref, q_ref, k_ref, c_ref, n_ref, m_ref, h_ref, x1_ref, *, tiles_per_seq):
    step = pl.program_id(0)
    old, new = _skew_slots(step, x1_ref)

    @pl.when(step % tiles_per_seq == 0)
    def _():
        hist_ref[...] = jnp.zeros_like(hist_ref)
        c_ref[...] = jnp.zeros_like(c_ref)
        n_ref[...] = jnp.zeros_like(n_ref)
        m_ref[...] = jnp.zeros_like(m_ref)

    def store(v):
        x1_ref[new] = v

    mlp = _mlp_pieces(lambda: x1_ref[old], fpre_ref, fpost_ref, w1_ref, w2_ref, out_ref)
    rec = _recur_pieces(x_ref, qk_ref, v_ref, o_ref, g_ref, cw_ref, cb_ref, gpost_ref, hnw_ref,
                        wout_ref, hist_ref, q_ref, k_ref, c_ref, n_ref, m_ref, h_ref, store)
    _interleave(mlp, rec)


def _mlstm_mlp_layer(x2, qk, v, o, g, conv_w, conv_b, gpost, hnw, w_out, fpre, fpost, w1, w2,
                     *, seq_len, ts=ROW_TILE):
    t, d = x2.shape
    dh = d // N_HEADS
    assert dh == MLSTM_CHUNK == LANES, "score and head-output tiles share one lane layout"
    n_tiles = t // ts
    tile = lambda width: pl.BlockSpec((ts, width), lambda i: (jnp.minimum(i, n_tiles - 1), 0))
    return pl.pallas_call(
        functools.partial(_mlstm_mlp_kernel, tiles_per_seq=seq_len // ts),
        grid=(n_tiles + 1,),
        in_specs=[
            tile(d), tile(2 * d), tile(d), tile(d), tile(2 * LANES),
            _resident(conv_w.shape), _resident(conv_b.shape),
            _resident((1, d)), _resident((1, d)), _resident(w_out.shape),
            _resident((1, d)), _resident((1, d)), _resident(w1.shape), _resident(w2.shape),
        ],
        out_specs=pl.BlockSpec((ts, d), lambda i: (jnp.maximum(i - 1, 0), 0)),
        out_shape=jax.ShapeDtypeStruct((t, d), F32),
        scratch_shapes=[
            pltpu.VMEM((CONV_HALO, 2 * d), F32),
            pltpu.VMEM((ts, d), BF16),
            pltpu.VMEM((ts, d), F32),
            pltpu.VMEM((N_HEADS, dh, dh), F32),
            pltpu.VMEM((N_HEADS, dh), F32),
            pltpu.VMEM((1, LANES), F32),
            pltpu.VMEM((ts, d), F32),
            pltpu.VMEM((2, ts, d), F32),
        ],
        compiler_params=pltpu.CompilerParams(
            dimension_semantics=("arbitrary",), vmem_limit_bytes=VMEM_LIMIT_BYTES),
        name="mlstm_mlp_layer",
    )(x2, qk, v, o, g, conv_w, conv_b, gpost, hnw, w_out, fpre, fpost, w1, w2)


def _pad_gates(w_in, i_bias, f_bias, d):
    h = N_HEADS
    wg = jnp.zeros((d, 2 * LANES), F32)
    wg = wg.at[:, :h].set(w_in[:, 4 * d:4 * d + h])
    wg = wg.at[:, LANES:LANES + h].set(w_in[:, 4 * d + h:])
    gb = jnp.zeros((1, 2 * LANES), F32)
    gb = gb.at[0, :h].set(i_bias)
    gb = gb.at[0, LANES:LANES + h].set(f_bias)
    return wg.astype(BF16), gb


def kernel(x, mix_pre_g, mix_post_g, ffn_pre_g, ffn_post_g, pool_w, pool_b, pool_scale,
           mlstm_w_in, mlstm_conv_w, mlstm_conv_b, mlstm_i_bias, mlstm_f_bias,
           mlstm_head_norm_w, mlstm_w_out, mlp_w1, mlp_w2):
    bsz, s, d = x.shape
    depth = mix_pre_g.shape[0]
    d_ff = mlp_w1.shape[2]
    row = lambda a: a.reshape(1, -1)
    n_groups, cg = pool_w.shape[1], pool_w.shape[2]
    pool_w_rows = pool_w.reshape(pool_w.shape[0], n_groups * cg, cg)
    mlp16 = {0: (mlp_w1[0].astype(BF16), mlp_w2[0].astype(BF16))}
    pool16 = {0: pool_w_rows[0].astype(BF16)}
    mlstm16 = {}
    x2 = x.reshape(bsz * s, d)
    for i in range(depth):
        j = i // 2
        nxt = i + 1 < depth
        mlp_casts = [(mlp_w1, i + 1, d_ff), (mlp_w2, i + 1, d)] if nxt else []
        mlp_args = (row(ffn_pre_g[i]), row(ffn_post_g[i])) + mlp16[i]
        if i % 2 == 0:
            casts = ([(mlstm_w_in, j, 4 * d), (mlstm_w_out, j, d)] if nxt else []) + mlp_casts
            x2, cast = _pool_mlp_layer(
                x2, row(mix_pre_g[i]), row(mix_post_g[i]), pool16[j].reshape(n_groups, cg, cg),
                row(pool_b[j]), row(pool_scale[j]), *mlp_args, seq_len=s, casts=casts)
            if nxt:
                mlstm16[j], mlp16[i + 1] = (cast[0], cast[1]), (cast[2], cast[3])
        else:
            w_in16, w_out16 = mlstm16[j]
            w_gate, gate_bias = _pad_gates(mlstm_w_in[j], mlstm_i_bias[j], mlstm_f_bias[j], d)
            casts = ([(pool_w_rows, j + 1, cg)] + mlp_casts) if nxt else []
            (qk, v, o, g), cast = _mlstm_proj_layer(
                x2, row(mix_pre_g[i]), w_in16, w_gate, gate_bias, casts=casts)
            if nxt:
                pool16[j + 1], mlp16[i + 1] = cast[0], (cast[1], cast[2])
            x2 = _mlstm_mlp_layer(x2, qk, v, o, g, mlstm_conv_w[j], row(mlstm_conv_b[j]),
                                  row(mix_post_g[i]), row(mlstm_head_norm_w[j]), w_out16,
                                  *mlp_args, seq_len=s)
    return x2.reshape(bsz, s, d)
```

```python
import functools

import jax
import jax.numpy as jnp
from jax import lax
from jax.experimental import pallas as pl
from jax.experimental.pallas import tpu as pltpu

F32 = jnp.float32
BF16 = jnp.bfloat16

RMS_EPS = 1e-6
POOL_WINDOWS = (2, 4, 8, 16)
N_HEADS = 8
CONV_WIDTH = 4

LANES = 128
SUBLANES = 8
VMEM_LIMIT_BYTES = 60 * 1024 * 1024

POOL_HALO = 16
CONV_HALO = SUBLANES
MLSTM_CHUNK = 128
ROW_TILE = 512
MLP_CHUNK = 512
PROJ_BLOCK = 256
CAST_CHUNKS = 64


def _rms(x, g):
    ms = jnp.mean(x * x, axis=-1, keepdims=True)
    return x * lax.rsqrt(ms + RMS_EPS) * g


def _resident(shape):
    nd = len(shape)
    return pl.BlockSpec(shape, lambda *_: (0,) * nd, pipeline_mode=pl.Buffered(1))


def _interleave(*streams):
    n = max(len(s) for s in streams)
    pos = [0] * len(streams)
    for i in range(n):
        for k, s in enumerate(streams):
            upto = -(-(i + 1) * len(s) // n)
            while pos[k] < upto:
                s[pos[k]]()
                pos[k] += 1


def _with_casts(body, n_in, n_out, n_cast):
    def kernel(*refs):
        a, b, c = n_in + n_cast, n_in + n_cast + n_out, n_in + 2 * n_cast + n_out
        for src, dst in zip(refs[n_in:a], refs[b:c]):
            dst[...] = src[:, :dst.shape[1]].astype(BF16)
        body(*refs[:n_in], *refs[a:b], *refs[c:])
    return kernel


def _cast_specs(casts, n_steps):
    in_specs, out_specs, out_shapes, args = [], [], [], []
    for stacked, layer, cols_out in casts:
        _, rows, cols = stacked.shape
        n_chunks = min(CAST_CHUNKS, n_steps)
        chunk = rows // n_chunks
        pick = lambda i, n=n_chunks: jnp.minimum(i, n - 1)
        assert chunk * n_chunks == rows and chunk % 16 == 0
        in_specs.append(pl.BlockSpec((None, chunk, cols), lambda i, l=layer: (l, pick(i), 0)))
        out_specs.append(pl.BlockSpec((chunk, cols_out), lambda i: (pick(i), 0)))
        out_shapes.append(jax.ShapeDtypeStruct((rows, cols_out), BF16))
        args.append(stacked)
    return in_specs, out_specs, out_shapes, args


def _mlp_pieces(load_x, gpre_ref, gpost_ref, w1_ref, w2_ref, out_ref):
    st = {}
    d_ff = w1_ref.shape[1]

    def head():
        st["hn"] = _rms(load_x(), gpre_ref[...]).astype(BF16)
        st["acc"] = None

    def chunk(c):
        def run():
            cols = slice(c * MLP_CHUNK, (c + 1) * MLP_CHUNK)
            h = jnp.dot(st["hn"], w1_ref[:, cols], preferred_element_type=F32)
            a = jnp.maximum(h, 0.0)
            a = (a * a).astype(BF16)
            p = jnp.dot(a, w2_ref[cols, :], preferred_element_type=F32)
            st["acc"] = p if st["acc"] is None else st["acc"] + p
        return run

    def tail():
        out_ref[...] = load_x() + _rms(st["acc"], gpost_ref[...])

    return [head] + [chunk(c) for c in range(d_ff // MLP_CHUNK)] + [tail]


def _skew_slots(step, x1_ref):
    new = step % 2

    @pl.when(step == 0)
    def _():
        x1_ref[1] = jnp.zeros(x1_ref.shape[1:], F32)

    return 1 - new, new


def _pool_pieces(x_ref, gpre_ref, gpost_ref, w_ref, b_ref, scale_ref, halo_ref, store, first_row):
    st = {"parts": []}
    ts, d = x_ref.shape
    cg = d // len(POOL_WINDOWS)

    def head():
        x = x_ref[...]
        hn = _rms(x, gpre_ref[...])
        st["hn"] = hn
        st["ext"] = jnp.concatenate([halo_ref[...], hn], axis=0)
        halo_ref[...] = hn[ts - POOL_HALO:, :]
        st["t"] = (first_row + lax.broadcasted_iota(jnp.int32, (ts, 1), 0)).astype(F32)

    def group(g, w):
        def run():
            e = st["ext"][:, g * cg:(g + 1) * cg]
            sh = 1
            while sh < w:
                e = e + pltpu.roll(e, sh, axis=0)
                sh *= 2
            inv_cnt = 1.0 / jnp.minimum(st["t"] + 1.0, float(w))
            pooled = e[POOL_HALO:, :] * inv_cnt - st["hn"][:, g * cg:(g + 1) * cg]
            st["parts"].append(jnp.dot(pooled.astype(BF16), w_ref[g], preferred_element_type=F32))
        return run

    def tail():
        y = jnp.concatenate(st["parts"], axis=-1)
        y = (y + b_ref[...]) * scale_ref[...]
        store(x_ref[...] + _rms(y, gpost_ref[...]))

    return [head] + [group(g, w) for g, w in enumerate(POOL_WINDOWS)] + [tail]


def _pool_mlp_kernel(x_ref, gpre_ref, gpost_ref, w_ref, b_ref, scale_ref,
                     fpre_ref, fpost_ref, w1_ref, w2_ref, out_ref, halo_ref, x1_ref,
                     *, tiles_per_seq):
    step = pl.program_id(0)
    ts = x_ref.shape[0]
    old, new = _skew_slots(step, x1_ref)

    @pl.when(step % tiles_per_seq == 0)
    def _():
        halo_ref[...] = jnp.zeros_like(halo_ref)

    def store(v):
        x1_ref[new] = v

    mlp = _mlp_pieces(lambda: x1_ref[old], fpre_ref, fpost_ref, w1_ref, w2_ref, out_ref)
    pool = _pool_pieces(x_ref, gpre_ref, gpost_ref, w_ref, b_ref, scale_ref, halo_ref, store,
                        (step % tiles_per_seq) * ts)
    _interleave(mlp, pool)


def _pool_mlp_layer(x2, gpre, gpost, w, b, scale, fpre, fpost, w1, w2, *, seq_len, casts=(),
                    ts=ROW_TILE):
    t, d = x2.shape
    n_tiles = t // ts
    c_in, c_out, c_shapes, c_args = _cast_specs(casts, n_tiles + 1)
    outs = pl.pallas_call(
        _with_casts(functools.partial(_pool_mlp_kernel, tiles_per_seq=seq_len // ts),
                    10, 1, len(casts)),
        grid=(n_tiles + 1,),
        in_specs=[
            pl.BlockSpec((ts, d), lambda i: (jnp.minimum(i, n_tiles - 1), 0)),
            _resident((1, d)), _resident((1, d)), _resident(w.shape),
            _resident((1, d)), _resident((1, d)),
            _resident((1, d)), _resident((1, d)), _resident(w1.shape), _resident(w2.shape),
        ] + c_in,
        out_specs=[pl.BlockSpec((ts, d), lambda i: (jnp.maximum(i - 1, 0), 0))] + c_out,
        out_shape=[jax.ShapeDtypeStruct((t, d), F32)] + c_shapes,
        scratch_shapes=[
            pltpu.VMEM((POOL_HALO, d), F32),
            pltpu.VMEM((2, ts, d), F32),
        ],
        compiler_params=pltpu.CompilerParams(
            dimension_semantics=("arbitrary",), vmem_limit_bytes=VMEM_LIMIT_BYTES),
        name="pool_mlp_layer",
    )(x2, gpre, gpost, w, b, scale, fpre, fpost, w1, w2, *c_args)
    return outs[0], outs[1:]


def _mlstm_proj_kernel(x_ref, gpre_ref, w_ref, wg_ref, gb_ref, qk_ref, v_ref, o_ref, g_ref):
    d = x_ref.shape[1]
    hn = _rms(x_ref[...], gpre_ref[...]).astype(BF16)
    for blk in range(4 * d // PROJ_BLOCK):
        part, off = divmod(blk * PROJ_BLOCK, d)
        cols = slice(blk * PROJ_BLOCK, (blk + 1) * PROJ_BLOCK)
        oc = slice(off, off + PROJ_BLOCK)
        pre = jnp.dot(hn, w_ref[:, cols], preferred_element_type=F32)
        if part < 2:
            qk_ref[:, cols] = pre
        elif part == 2:
            v_ref[:, oc] = pre.astype(BF16)
        else:
            o_ref[:, oc] = pre
    g_ref[...] = jnp.dot(hn, wg_ref[...], preferred_element_type=F32) + gb_ref[...]


def _mlstm_proj_layer(x2, gpre, w_main, w_gate, gate_bias, *, casts=(), ts=2 * ROW_TILE):
    t, d = x2.shape
    tile = lambda width: pl.BlockSpec((ts, width), lambda i: (i, 0))
    c_in, c_out, c_shapes, c_args = _cast_specs(casts, t // ts)
    outs = pl.pallas_call(
        _with_casts(_mlstm_proj_kernel, 5, 4, len(casts)),
        grid=(t // ts,),
        in_specs=[tile(d), _resident((1, d)), _resident(w_main.shape), _resident(w_gate.shape),
                  _resident(gate_bias.shape)] + c_in,
        out_specs=[tile(2 * d), tile(d), tile(d), tile(2 * LANES)] + c_out,
        out_shape=[
            jax.ShapeDtypeStruct((t, 2 * d), F32),
            jax.ShapeDtypeStruct((t, d), BF16),
            jax.ShapeDtypeStruct((t, d), F32),
            jax.ShapeDtypeStruct((t, 2 * LANES), F32),
        ] + c_shapes,
        compiler_params=pltpu.CompilerParams(
            dimension_semantics=("arbitrary",), vmem_limit_bytes=VMEM_LIMIT_BYTES),
        name="mlstm_proj_layer",
    )(x2, gpre, w_main, w_gate, gate_bias, *c_args)
    return outs[:4], outs[4:]


def _scan_rows(x, op, fill):
    n = x.shape[0]
    row = lax.broadcasted_iota(jnp.int32, x.shape, 0)
    sh = 1
    while sh < n:
        x = op(x, jnp.where(row >= sh, pltpu.roll(x, sh, axis=0), fill))
        sh *= 2
    return x


def _log_sigmoid(x):
    return -(jnp.maximum(-x, 0.0) + jnp.log1p(jnp.exp(-jnp.abs(x))))


def _chunk_gates(g_ref, m_ref, rows):
    chunk = rows.stop - rows.start
    ig = g_ref[rows, :LANES]
    b = _scan_rows(_log_sigmoid(g_ref[rows, LANES:]), jnp.add, 0.0)
    r = ig - b
    pm = _scan_rows(r, jnp.maximum, -jnp.inf)
    m_prev = m_ref[...]
    u = jnp.maximum(pm, m_prev)
    g = b[chunk - 1:chunk, :]
    a = g + r
    m_new = jnp.maximum(g + m_prev, jnp.max(a, axis=0, keepdims=True))
    m_ref[...] = m_new
    return dict(
        u=u,
        m_prev=m_prev,
        inv_floor=jnp.exp(-(b + u)),
        decay=jnp.exp(g + m_prev - m_new),
        wts_t=jnp.exp(a - m_new).T,
        r_t=r.T,
    )


def _head_scores(h, rows, gt, causal, q_ref, k_ref):
    dh = q_ref.shape[-1] // N_HEADS
    cs = slice(h * dh, (h + 1) * dh)
    hl = slice(h, h + 1)
    s = lax.dot_general(q_ref[rows, cs], k_ref[rows, cs].astype(BF16), (((1,), (1,)), ((), ())),
                        preferred_element_type=F32)
    u_bc = jnp.broadcast_to(gt["u"][:, hl], s.shape)
    arg = jnp.where(causal, gt["r_t"][hl, :] - u_bc, -jnp.inf)
    return s * jnp.exp(arg), jnp.exp(gt["m_prev"][:, hl] - u_bc)


def _head_finish(h, rows, gt, scored, q_ref, k_ref, v_ref, c_ref, n_ref, h_ref):
    s, inter = scored
    dh = q_ref.shape[-1] // N_HEADS
    cs = slice(h * dh, (h + 1) * dh)
    hl = slice(h, h + 1)
    qh = q_ref[rows, cs]
    kf = k_ref[rows, cs]
    vh = v_ref[rows, cs]
    c_old = c_ref[h]
    n_old = n_ref[hl, :]
    num = (jnp.dot(s.astype(BF16), vh, preferred_element_type=F32)
           + inter * jnp.dot(qh, c_old.astype(BF16), preferred_element_type=F32))
    qn = qh.astype(F32) * n_old.astype(BF16).astype(F32)
    den = jnp.sum(s + inter * qn, axis=-1, keepdims=True)
    rinv = 1.0 / jnp.maximum(jnp.abs(den), gt["inv_floor"][:, hl])
    ms = jnp.mean(num * num, axis=-1, keepdims=True)
    h_ref[rows, cs] = num * (rinv * lax.rsqrt(rinv * rinv * ms + RMS_EPS))
    wts_row = gt["wts_t"][hl, :]
    kw_t = (kf.T * wts_row).astype(BF16)
    decay_h = gt["decay"][:, hl]
    c_ref[h] = decay_h * c_old + jnp.dot(kw_t, vh, preferred_element_type=F32)
    n_upd = jnp.dot(gt["wts_t"][:2 * SUBLANES, :].astype(BF16), kf.astype(BF16),
                    preferred_element_type=F32)
    n_ref[hl, :] = decay_h * n_old + n_upd[hl, :]


def _conv_block(blk, qk_ref, cw_ref, cb_ref, hist_ref, q_ref, k_ref):
    ts, d = q_ref.shape
    dh = d // N_HEADS
    part, off = divmod(blk * PROJ_BLOCK, d)
    cols = slice(blk * PROJ_BLOCK, (blk + 1) * PROJ_BLOCK)
    oc = slice(off, off + PROJ_BLOCK)
    pre = qk_ref[:, cols]
    ext = jnp.concatenate([hist_ref[:, cols], pre], axis=0)
    hist_ref[:, cols] = pre[ts - CONV_HALO:, :]
    assert CONV_WIDTH == 4
    w = [cw_ref[tap:tap + 1, cols] for tap in range(CONV_WIDTH)]
    ext1 = pltpu.roll(ext, 1, axis=0)
    far = pltpu.roll(w[1] * ext + w[0] * ext1, 2, axis=0)
    conv = cb_ref[:, cols] + far[CONV_HALO:, :] + (w[3] * ext + w[2] * ext1)[CONV_HALO:, :]
    act = conv * jax.nn.sigmoid(conv)
    if part == 0:
        q_ref[:, oc] = act.astype(BF16)
    else:
        k_ref[:, oc] = act * (dh ** -0.5)


def _recur_pieces(x_ref, qk_ref, v_ref, o_ref, g_ref, cw_ref, cb_ref, gpost_ref, hnw_ref, wout_ref,
                  hist_ref, q_ref, k_ref, c_ref, n_ref, m_ref, h_ref, store):
    ts, d = x_ref.shape
    chunk = MLSTM_CHUNK
    causal = (lax.broadcasted_iota(jnp.int32, (chunk, chunk), 0)
              >= lax.broadcasted_iota(jnp.int32, (chunk, chunk), 1))
    units = [(c, h) for c in range(ts // chunk) for h in range(N_HEADS)]
    rows_of = lambda c: slice(c * chunk, (c + 1) * chunk)
    gates, scores = {}, {}

    def prepare(u):
        if u >= len(units):
            return
        c, h = units[u]
        if c not in gates:
            gates[c] = _chunk_gates(g_ref, m_ref, rows_of(c))
        scores[u] = _head_scores(h, rows_of(c), gates[c], causal, q_ref, k_ref)

    def unit(u):
        def run():
            prepare(u + 1)
            c, h = units[u]
            _head_finish(h, rows_of(c), gates[c], scores.pop(u), q_ref, k_ref, v_ref,
                         c_ref, n_ref, h_ref)
        return run

    def tail():
        gated = jax.nn.sigmoid(o_ref[...]) * (h_ref[...] * hnw_ref[...])
        y = jnp.dot(gated.astype(BF16), wout_ref[...], preferred_element_type=F32)
        store(x_ref[...] + _rms(y, gpost_ref[...]))

    conv = [functools.partial(_conv_block, blk, qk_ref, cw_ref, cb_ref, hist_ref, q_ref, k_ref)
            for blk in range(2 * d // PROJ_BLOCK)]
    return conv + [lambda: prepare(0)] + [unit(u) for u in range(len(units))] + [tail]


def _mlstm_mlp_kernel(x_ref, qk_ref, v_ref, o_ref, g_ref, cw_ref, cb_ref, gpost_ref, hnw_ref,
                      wout_ref, fpre_ref, fpost_ref, w1_ref, w2_ref, out_ref,
                      hist_ref, q_ref, k_ref, c_ref, n_ref, m_ref, h_ref, x1_ref, *, tiles_per_seq):
    step = pl.program_id(0)
    old, new = _skew_slots(step, x1_ref)

    @pl.when(step % tiles_per_seq == 0)
    def _():
        hist_ref[...] = jnp.zeros_like(hist_ref)
        c_ref[...] = jnp.zeros_like(c_ref)
        n_ref[...] = jnp.zeros_like(n_ref)
        m_ref[...] = jnp.zeros_like(m_ref)

    def store(v):
        x1_ref[new] = v

    mlp = _mlp_pieces(lambda: x1_ref[old], fpre_ref, fpost_ref, w1_ref, w2_ref, out_ref)
    rec = _recur_pieces(x_ref, qk_ref, v_ref, o_ref, g_ref, cw_ref, cb_ref, gpost_ref, hnw_ref,
                        wout_ref, hist_ref, q_ref, k_ref, c_ref, n_ref, m_ref, h_ref, store)
    _interleave(mlp, rec)


def _mlstm_mlp_layer(x2, qk, v, o, g, conv_w, conv_b, gpost, hnw, w_out, fpre, fpost, w1, w2,
                     *, seq_len, ts=ROW_TILE):
    t, d = x2.shape
    dh = d // N_HEADS
    assert dh == MLSTM_CHUNK == LANES, "score and head-output tiles share one lane layout"
    n_tiles = t // ts
    tile = lambda width: pl.BlockSpec((ts, width), lambda i: (jnp.minimum(i, n_tiles - 1), 0))
    return pl.pallas_call(
        functools.partial(_mlstm_mlp_kernel, tiles_per_seq=seq_len // ts),
        grid=(n_tiles + 1,),
        in_specs=[
            tile(d), tile(2 * d), tile(d), tile(d), tile(2 * LANES),
            _resident(conv_w.shape), _resident(conv_b.shape),
            _resident((1, d)), _resident((1, d)), _resident(w_out.shape),
            _resident((1, d)), _resident((1, d)), _resident(w1.shape), _resident(w2.shape),
        ],
        out_specs=pl.BlockSpec((ts, d), lambda i: (jnp.maximum(i - 1, 0), 0)),
        out_shape=jax.ShapeDtypeStruct((t, d), F32),
        scratch_shapes=[
            pltpu.VMEM((CONV_HALO, 2 * d), F32),
            pltpu.VMEM((ts, d), BF16),
            pltpu.VMEM((ts, d), F32),
            pltpu.VMEM((N_HEADS, dh, dh), F32),
            pltpu.VMEM((N_HEADS, dh), F32),
            pltpu.VMEM((1, LANES), F32),
            pltpu.VMEM((ts, d), F32),
            pltpu.VMEM((2, ts, d), F32),
        ],
        compiler_params=pltpu.CompilerParams(
            dimension_semantics=("arbitrary",), vmem_limit_bytes=VMEM_LIMIT_BYTES),
        name="mlstm_mlp_layer",
    )(x2, qk, v, o, g, conv_w, conv_b, gpost, hnw, w_out, fpre, fpost, w1, w2)


def _pad_gates(w_in, i_bias, f_bias, d):
    h = N_HEADS
    wg = jnp.zeros((d, 2 * LANES), F32)
    wg = wg.at[:, :h].set(w_in[:, 4 * d:4 * d + h])
    wg = wg.at[:, LANES:LANES + h].set(w_in[:, 4 * d + h:])
    gb = jnp.zeros((1, 2 * LANES), F32)
    gb = gb.at[0, :h].set(i_bias)
    gb = gb.at[0, LANES:LANES + h].set(f_bias)
    return wg.astype(BF16), gb


def kernel(x, mix_pre_g, mix_post_g, ffn_pre_g, ffn_post_g, pool_w, pool_b, pool_scale,
           mlstm_w_in, mlstm_conv_w, mlstm_conv_b, mlstm_i_bias, mlstm_f_bias,
           mlstm_head_norm_w, mlstm_w_out, mlp_w1, mlp_w2):
    bsz, s, d = x.shape
    depth = mix_pre_g.shape[0]
    d_ff = mlp_w1.shape[2]
    row = lambda a: a.reshape(1, -1)
    n_groups, cg = pool_w.shape[1], pool_w.shape[2]
    pool_w_rows = pool_w.reshape(pool_w.shape[0], n_groups * cg, cg)
    mlp16 = {0: (mlp_w1[0].astype(BF16), mlp_w2[0].astype(BF16))}
    pool16 = {0: pool_w_rows[0].astype(BF16)}
    mlstm16 = {}
    x2 = x.reshape(bsz * s, d)
    for i in range(depth):
        j = i // 2
        nxt = i + 1 < depth
        mlp_casts = [(mlp_w1, i + 1, d_ff), (mlp_w2, i + 1, d)] if nxt else []
        mlp_args = (row(ffn_pre_g[i]), row(ffn_post_g[i])) + mlp16[i]
        if i % 2 == 0:
            casts = ([(mlstm_w_in, j, 4 * d), (mlstm_w_out, j, d)] if nxt else []) + mlp_casts
            x2, cast = _pool_mlp_layer(
                x2, row(mix_pre_g[i]), row(mix_post_g[i]), pool16[j].reshape(n_groups, cg, cg),
                row(pool_b[j]), row(pool_scale[j]), *mlp_args, seq_len=s, casts=casts)
            if nxt:
                mlstm16[j], mlp16[i + 1] = (cast[0], cast[1]), (cast[2], cast[3])
        else:
            w_in16, w_out16 = mlstm16[j]
            w_gate, gate_bias = _pad_gates(mlstm_w_in[j], mlstm_i_bias[j], mlstm_f_bias[j], d)
            casts = ([(pool_w_rows, j + 1, cg)] + mlp_casts) if nxt else []
            (qk, v, o, g), cast = _mlstm_proj_layer(
                x2, row(mix_pre_g[i]), w_in16, w_gate, gate_bias, casts=casts)
            if nxt:
                pool16[j + 1], mlp16[i + 1] = cast[0], (cast[1], cast[2])
            x2 = _mlstm_mlp_layer(x2, qk, v, o, g, mlstm_conv_w[j], row(mlstm_conv_b[j]),
                                  row(mix_post_g[i]), row(mlstm_head_norm_w[j]), w_out16,
                                  *mlp_args, seq_len=s)
    return x2.reshape(bsz, s, d)
```
